```python
import jax, jax.numpy as jnp
from jax import lax
import numpy as np


D_MODEL = 1024
BATCH = 8
SEQ = 2048
DEPTH = 2
DEC_BATCH = 128
DEC_SEQ = 1
PAST_LEN = 16384
PAGE_SIZE = 128

RW_HEAD = 64
RW_HEADS = D_MODEL // RW_HEAD
D_RW = RW_HEADS * RW_HEAD
R_DECAY = 64
R_AAA = 64
R_GATE = 128
GN_EPS = 64e-5
D_LRU = D_MODEL
LRU_BLOCKS = 16
LRU_BS = D_LRU // LRU_BLOCKS
CONV_W = 4
LRU_C = 8.0
N_MEM = 256
XA_HEADS = 4
XA_HEAD = D_MODEL // XA_HEADS
D_FF = 4 * D_MODEL
ALPHA = (2 * DEPTH) ** 0.25
BETA = (8 * DEPTH) ** -0.25
LN_EPS = 1e-5

C_R = 0
C_K = C_R + D_RW
C_V = C_K + D_RW
C_W = C_V + D_RW
C_A = C_W + R_DECAY
C_G = C_A + R_AAA
N_SHIFT = C_G + R_GATE
C_U = N_SHIFT
C_Y = C_U + D_LRU
C_GATE = C_Y + D_LRU
N_IN = C_GATE + 2 * D_MODEL

kernel_name = 'rwkv7_rglru_gated_hybrid_step'


def _layer_norm(x, g, b):
    xf = x.astype(jnp.float32)
    mu = jnp.mean(xf, -1, keepdims=True)
    var = jnp.mean(jnp.square(xf - mu), -1, keepdims=True)
    return ((xf - mu) * lax.rsqrt(var + LN_EPS)).astype(x.dtype) * g + b


def _rwkv7_scan(r, w, k, v, a, b, s0):
    def step(s, inp):
        r_t, w_t, k_t, v_t, a_t, b_t = inp
        sa = jnp.einsum('bhvk,bhk->bhv', s, a_t)
        s = s * w_t[:, :, None, :] + sa[..., None] * b_t[:, :, None, :] + v_t[..., None] * k_t[:, :, None, :]
        y = jnp.einsum('bhvk,bhk->bhv', s, r_t)
        return s, y
    xs = (jnp.moveaxis(r, 1, 0), jnp.moveaxis(w, 1, 0), jnp.moveaxis(k, 1, 0),
          jnp.moveaxis(v, 1, 0), jnp.moveaxis(a, 1, 0), jnp.moveaxis(b, 1, 0))
    s, ys = lax.scan(step, s0, xs)
    return jnp.moveaxis(ys, 0, 1), s


def _rwkv7_branch(p_rw, p_prev, s0, l, W):
    f32 = jnp.float32
    ps = p_rw + W['mu_shift'][l] * (p_prev - p_rw)
    r = ps[..., C_R:C_K]
    k = ps[..., C_K:C_V]
    v = ps[..., C_V:C_W]
    xw = ps[..., C_W:C_A]
    xa = ps[..., C_A:C_G]
    xg = ps[..., C_G:N_SHIFT]
    w = -jax.nn.softplus(-(W['rw_w0'][l] + jnp.tanh(xw) @ W['rw_w2'][l])) - 0.5
    a = jax.nn.sigmoid(W['rw_a0'][l] + xa @ W['rw_a2'][l])
    g = jax.nn.sigmoid(xg) @ W['rw_g2'][l]
    hs = p_rw.shape[:-1] + (RW_HEADS, RW_HEAD)
    r = r.reshape(hs).astype(f32)
    k = k.reshape(hs).astype(f32)
    v = v.reshape(hs).astype(f32)
    a = a.reshape(hs).astype(f32)
    decay = jnp.exp(-jnp.exp(w.astype(f32))).reshape(hs)
    kk = k * W['rw_k_k'][l]
    kk = kk * lax.rsqrt(jnp.maximum(jnp.sum(kk * kk, -1, keepdims=True), 1e-24))
    k = k * (1.0 + (a - 1.0) * W['rw_k_a'][l])
    y, s = _rwkv7_scan(r, decay, k, v, -kk, kk * a, s0.astype(f32))
    mu = jnp.mean(y, -1, keepdims=True)
    var = jnp.mean(jnp.square(y - mu), -1, keepdims=True)
    y = (y - mu) * lax.rsqrt(var + GN_EPS) * W['rw_lnx_g'][l] + W['rw_lnx_b'][l]
    y = y + jnp.sum(r * k * W['rw_r_k'][l], -1, keepdims=True) * v
    y = y.reshape(p_rw.shape[:-1] + (D_RW,)).astype(p_rw.dtype) * g
    return y @ W['rw_proj'][l], s.astype(s0.dtype)


def _linear_scan(a, b, h0):
    b = b.at[:, 0].add(a[:, 0] * h0)
    def comb(left, right):
        al, bl = left
        ar, br = right
        return al * ar, ar * bl + br
    _, h = lax.associative_scan(comb, (a, b), axis=1)
    return h


def _rglru_branch(u, y_in, buf, h0, l, W):
    f32 = jnp.float32
    T = u.shape[1]
    ext = jnp.concatenate([buf.astype(u.dtype), u], axis=1)
    cw = W['lru_conv_w'][l]
    xc = W['lru_conv_b'][l] + ext[:, 0:T] * cw[0]
    for j in range(1, CONV_W):
        xc = xc + ext[:, j:j + T] * cw[j]
    new_buf = ext[:, T:]
    blk = xc.reshape(xc.shape[:-1] + (LRU_BLOCKS, LRU_BS))
    gr = jax.nn.sigmoid(jnp.einsum('btnd,nde->btne', blk, W['lru_wa'][l]).reshape(xc.shape) + W['lru_ba'][l])
    gi = jax.nn.sigmoid(jnp.einsum('btnd,nde->btne', blk, W['lru_wx'][l]).reshape(xc.shape) + W['lru_bx'][l])
    log_a = (-LRU_C * jax.nn.softplus(-W['lru_lambda'][l]) * gr).astype(f32)
    a = jnp.exp(log_a)
    bterm = jnp.sqrt(-jnp.expm1(2.0 * log_a)) * (gi * xc).astype(f32)
    h = _linear_scan(a, bterm, h0.astype(f32))
    y = h.astype(u.dtype) * jax.nn.gelu(y_in)
    return y @ W['lru_proj'][l], h[:, -1].astype(h0.dtype), new_buf.astype(buf.dtype)


def _mem_attention(x, k, v, wq, wo):
    q = (x @ wq).reshape(x.shape[:-1] + (XA_HEADS, XA_HEAD))
    s = jnp.einsum('bthd,bmhd->bhtm', q, k.astype(q.dtype)).astype(jnp.float32) * (XA_HEAD ** -0.5)
    p = jax.nn.softmax(s, axis=-1).astype(x.dtype)
    o = jnp.einsum('bhtm,bmhd->bthd', p, v.astype(x.dtype)).reshape(x.shape)
    return o @ wo


def _trunk(x, mem_k, mem_v, s_rw, s_shift, s_h, s_conv, W):
    o_rw_st, o_shift_st, o_h_st, o_conv_st = [], [], [], []
    for l in range(DEPTH):
        p = x @ W['w_in'][l]
        p_rw = p[..., :N_SHIFT]
        p_prev = jnp.concatenate([s_shift[l][:, None].astype(p.dtype), p_rw[:, :-1]], axis=1)
        o_rw, s_rw_l = _rwkv7_branch(p_rw, p_prev, s_rw[l], l, W)
        o_lru, h_l, conv_l = _rglru_branch(p[..., C_U:C_Y], p[..., C_Y:C_GATE], s_conv[l], s_h[l], l, W)
        gates = jax.nn.sigmoid(p[..., C_GATE:].reshape(p.shape[:-1] + (2, D_MODEL)) + W['mix_gate_b'][l])
        mix = (gates[..., 0, :] * o_rw + gates[..., 1, :] * o_lru) @ W['w_out_mix'][l]
        x = _layer_norm(ALPHA * x + mix, W['ln1_g'][l], W['ln1_b'][l])
        xa = _mem_attention(x, mem_k[l], mem_v[l], W['xa_wq'][l], W['xa_wo'][l])
        x = _layer_norm(ALPHA * x + xa, W['ln2_g'][l], W['ln2_b'][l])
        hdn = jnp.square(jax.nn.relu(x @ W['mlp_up'][l]))
        x = _layer_norm(ALPHA * x + hdn @ W['mlp_down'][l], W['ln3_g'][l], W['ln3_b'][l])
        o_rw_st.append(s_rw_l)
        o_shift_st.append(p_rw[:, -1].astype(s_shift.dtype))
        o_h_st.append(h_l)
        o_conv_st.append(conv_l)
    return x, (jnp.stack(o_rw_st, 0), jnp.stack(o_shift_st, 0), jnp.stack(o_h_st, 0), jnp.stack(o_conv_st, 0))


def setup_inputs(seed: int = 0) -> dict:
    key = jax.random.key(seed)
    ks = iter(jax.random.split(key, 64))
    def nrm(shape, scale):
        return jax.random.normal(next(ks), shape, jnp.float32) * scale
    def unif(shape, lo, hi):
        return jax.random.uniform(next(ks), shape, jnp.float32, lo, hi)
    L = DEPTH
    lam_s = unif((L, D_LRU), 0.9, 0.999) ** (1.0 / LRU_C)
    lru_lambda = jnp.log(lam_s) - jnp.log1p(-lam_s)
    return {
        'x_prompt': nrm((BATCH, SEQ, D_MODEL), 1.0),
        'x_sample': nrm((DEC_BATCH, DEC_SEQ, D_MODEL), 1.0),
        'mem_prompt': nrm((BATCH, N_MEM, D_MODEL), 1.0),
        'cache_mem_k': nrm((L, DEC_BATCH, N_MEM, XA_HEADS, XA_HEAD), 1.0),
        'cache_mem_v': nrm((L, DEC_BATCH, N_MEM, XA_HEADS, XA_HEAD), BETA),
        'state_rwkv': nrm((L, DEC_BATCH, RW_HEADS, RW_HEAD, RW_HEAD), 0.3),
        'state_rwkv_shift': nrm((L, DEC_BATCH, N_SHIFT), 1.0),
        'state_lru_h': nrm((L, DEC_BATCH, D_LRU), 0.5),
        'state_lru_conv': nrm((L, DEC_BATCH, CONV_W - 1, D_LRU), 1.0),
        'w_in': nrm((L, D_MODEL, N_IN), D_MODEL ** -0.5),
        'mu_shift': unif((L, N_SHIFT), 0.0, 1.0),
        'rw_w0': unif((L, D_RW), -6.0, -1.0),
        'rw_w2': nrm((L, R_DECAY, D_RW), 0.5 * R_DECAY ** -0.5),
        'rw_a0': nrm((L, D_RW), 0.5),
        'rw_a2': nrm((L, R_AAA, D_RW), 0.5 * R_AAA ** -0.5),
        'rw_g2': nrm((L, R_GATE, D_RW), R_GATE ** -0.5),
        'rw_k_k': 0.85 + nrm((L, RW_HEADS, RW_HEAD), 0.05),
        'rw_k_a': 1.0 + nrm((L, RW_HEADS, RW_HEAD), 0.05),
        'rw_r_k': nrm((L, RW_HEADS, RW_HEAD), 0.1),
        'rw_lnx_g': 1.0 + nrm((L, RW_HEADS, RW_HEAD), 0.05),
        'rw_lnx_b': nrm((L, RW_HEADS, RW_HEAD), 0.01),
        'rw_proj': nrm((L, D_RW, D_MODEL), D_RW ** -0.5),
        'lru_conv_w': nrm((L, CONV_W, D_LRU), CONV_W ** -0.5),
        'lru_conv_b': nrm((L, D_LRU), 0.01),
        'lru_wa': nrm((L, LRU_BLOCKS, LRU_BS, LRU_BS), LRU_BS ** -0.5),
        'lru_ba': nrm((L, D_LRU), 0.01),
        'lru_wx': nrm((L, LRU_BLOCKS, LRU_BS, LRU_BS), LRU_BS ** -0.5),
        'lru_bx': nrm((L, D_LRU), 0.01),
        'lru_lambda': lru_lambda,
        'lru_proj': nrm((L, D_LRU, D_MODEL), D_LRU ** -0.5),
        'mix_gate_b': nrm((L, 2, D_MODEL), 0.01),
        'w_out_mix': nrm((L, D_MODEL, D_MODEL), BETA * D_MODEL ** -0.5),
        'ln1_g': 1.0 + nrm((L, D_MODEL), 0.05),
        'ln1_b': nrm((L, D_MODEL), 0.01),
        'xa_wq': nrm((L, D_MODEL, D_MODEL), D_MODEL ** -0.5),
        'xa_wk': nrm((L, D_MODEL, D_MODEL), D_MODEL ** -0.5),
        'xa_wv': nrm((L, D_MODEL, D_MODEL), BETA * D_MODEL ** -0.5),
        'xa_wo': nrm((L, D_MODEL, D_MODEL), BETA * D_MODEL ** -0.5),
        'ln2_g': 1.0 + nrm((L, D_MODEL), 0.05),
        'ln2_b': nrm((L, D_MODEL), 0.01),
        'mlp_up': nrm((L, D_MODEL, D_FF), D_MODEL ** -0.5),
        'mlp_down': nrm((L, D_FF, D_MODEL), BETA * D_FF ** -0.5),
        'ln3_g': 1.0 + nrm((L, D_MODEL), 0.05),
        'ln3_b': nrm((L, D_MODEL), 0.01),
    }


def reference(x_prompt, x_sample, mem_prompt, cache_mem_k, cache_mem_v, state_rwkv, state_rwkv_shift,
              state_lru_h, state_lru_conv, w_in, mu_shift, rw_w0, rw_w2, rw_a0, rw_a2, rw_g2, rw_k_k,
              rw_k_a, rw_r_k, rw_lnx_g, rw_lnx_b, rw_proj, lru_conv_w, lru_conv_b, lru_wa, lru_ba, lru_wx,
              lru_bx, lru_lambda, lru_proj, mix_gate_b, w_out_mix, ln1_g, ln1_b, xa_wq, xa_wk, xa_wv,
              xa_wo, ln2_g, ln2_b, mlp_up, mlp_down, ln3_g, ln3_b):
    W = dict(w_in=w_in, mu_shift=mu_shift, rw_w0=rw_w0, rw_w2=rw_w2, rw_a0=rw_a0, rw_a2=rw_a2,
             rw_g2=rw_g2, rw_k_k=rw_k_k, rw_k_a=rw_k_a, rw_r_k=rw_r_k, rw_lnx_g=rw_lnx_g,
             rw_lnx_b=rw_lnx_b, rw_proj=rw_proj, lru_conv_w=lru_conv_w, lru_conv_b=lru_conv_b,
             lru_wa=lru_wa, lru_ba=lru_ba, lru_wx=lru_wx, lru_bx=lru_bx, lru_lambda=lru_lambda,
             lru_proj=lru_proj, mix_gate_b=mix_gate_b, w_out_mix=w_out_mix, ln1_g=ln1_g, ln1_b=ln1_b,
             xa_wq=xa_wq, xa_wo=xa_wo, ln2_g=ln2_g, ln2_b=ln2_b, mlp_up=mlp_up, mlp_down=mlp_down,
             ln3_g=ln3_g, ln3_b=ln3_b)
    bp = x_prompt.shape[0]
    dt = x_prompt.dtype
    mem_k_p = jnp.einsum('bmd,lde->lbme', mem_prompt, xa_wk).reshape(DEPTH, bp, N_MEM, XA_HEADS, XA_HEAD)
    mem_v_p = jnp.einsum('bmd,lde->lbme', mem_prompt, xa_wv).reshape(DEPTH, bp, N_MEM, XA_HEADS, XA_HEAD)
    z_rw = jnp.zeros((DEPTH, bp, RW_HEADS, RW_HEAD, RW_HEAD), dt)
    z_shift = jnp.zeros((DEPTH, bp, N_SHIFT), dt)
    z_h = jnp.zeros((DEPTH, bp, D_LRU), dt)
    z_conv = jnp.zeros((DEPTH, bp, CONV_W - 1, D_LRU), dt)
    y_prompt, (p_rw, p_shift, p_h, p_conv) = _trunk(x_prompt, mem_k_p, mem_v_p, z_rw, z_shift, z_h, z_conv, W)
    y_sample, (s_rw, s_shift, s_h, s_conv) = _trunk(x_sample, cache_mem_k, cache_mem_v, state_rwkv,
                                                    state_rwkv_shift, state_lru_h, state_lru_conv, W)
    return (y_prompt, y_sample, p_rw, p_shift, p_h, p_conv, mem_k_p, mem_v_p, s_rw, s_shift, s_h, s_conv)
```

```python
import functools

import jax
import jax.numpy as jnp
from jax import lax
from jax.experimental import pallas as pl
from jax.experimental.pallas import tpu as pltpu

F32 = jnp.float32
BF16 = jnp.bfloat16

D_MODEL = 1024
DEPTH = 2
RW_HEAD = 64
RW_HEADS = 16
N_PAIR = 8
LANES = 128
R_GATE = 128
GN_EPS = 64e-5
LRU_C = 8.0
CONV_W = 4
N_MEM = 256
XA_HEADS = 4
XA_HEAD = 256
D_FF = 4096
ALPHA = (2 * DEPTH) ** 0.25
LN_EPS = 1e-5
N_SHIFT = 3328
C_UY = 2048
CHUNK = 64
SUB = 16
VMEM_LIMIT = 56 * 1024 * 1024

_NT = (((1,), (1,)), ((), ()))
_TN = (((0,), (0,)), ((), ()))


def _params(sem):
    return pltpu.CompilerParams(dimension_semantics=sem, vmem_limit_bytes=VMEM_LIMIT)


def _sigmoid(x):
    return 1.0 / (1.0 + jnp.exp(-x))


def _softplus(x):
    return jnp.maximum(x, 0.0) + jnp.log(1.0 + jnp.exp(-jnp.abs(x)))


def _gelu_tanh(x):
    return 0.5 * x * (1.0 + jnp.tanh(0.7978845608028654 * (x + 0.044715 * (x * x * x))))


def _layer_norm(z, g, b):
    mu = jnp.mean(z, -1, keepdims=True)
    zc = z - mu
    var = jnp.mean(zc * zc, -1, keepdims=True)
    return zc * lax.rsqrt(var + LN_EPS) * g + b


def _dot(a, b, dims=None):
    a = a.astype(BF16)
    b = b.astype(BF16)
    if dims is None:
        return jnp.dot(a, b, preferred_element_type=F32)
    return lax.dot_general(a, b, dims, preferred_element_type=F32)


def _split(a):
    hi = a.astype(BF16)
    lo = (a - hi.astype(F32)).astype(BF16)
    return hi, lo


def _dot3(a, b, dims=None):
    ah, al = _split(a)
    bh, bl = _split(b)
    return _dot(ah, bh, dims) + (_dot(ah, bl, dims) + _dot(al, bh, dims))


def _dot2x(a, b_exact):
    ah, al = _split(a)
    return _dot(ah, b_exact) + _dot(al, b_exact)


def _pair_ones():
    r = lax.broadcasted_iota(jnp.int32, (LANES, LANES), 0) // RW_HEAD
    c = lax.broadcasted_iota(jnp.int32, (LANES, LANES), 1) // RW_HEAD
    return jnp.where(r == c, 1.0, 0.0).astype(BF16)


def _segsum(x, ones):
    cols = [_dot2x(x[:, p * LANES:(p + 1) * LANES], ones) for p in range(x.shape[1] // LANES)]
    return jnp.concatenate(cols, axis=1) if len(cols) > 1 else cols[0]


def _blockdot(x, w_ref):
    cols = [_dot(x[:, p * LANES:(p + 1) * LANES], w_ref[p]) for p in range(N_PAIR)]
    return jnp.concatenate(cols, axis=1)


def _rwkv_elem(ps, w, ones):
    r = ps[:, 0:1024]
    k = ps[:, 1024:2048]
    v = ps[:, 2048:3072]
    slab = ps[:, 3072:3200]
    xg = ps[:, 3200:3328]
    wlog = -_softplus(-(w['w0'][...] + _dot(jnp.tanh(slab), w['w2'][...]))) - 0.5
    ld = -jnp.exp(wlog)
    a_sig = _sigmoid(w['a0'][...] + _dot(slab, w['a2'][...]))
    g = _dot(_sigmoid(xg), w['g2'][...])
    kk = k * w['k_k'][...]
    kk = kk * lax.rsqrt(jnp.maximum(_segsum(kk * kk, ones), 1e-24))
    k_mod = k * (1.0 + (a_sig - 1.0) * w['k_a'][...])
    return r, k_mod, v, ld, -kk, kk * a_sig, g


def _lru_elem(xc, y_in, w):
    gr = _sigmoid(_blockdot(xc, w['wa']) + w['ba'][...])
    gi = _sigmoid(_blockdot(xc, w['wx']) + w['bx'][...])
    log_a = -LRU_C * _softplus(-w['lam'][...]) * gr
    a = jnp.exp(log_a)
    bterm = jnp.sqrt(1.0 - jnp.exp(2.0 * log_a)) * (gi * xc)
    return a, bterm, _gelu_tanh(y_in)


_RW_NAMES = ('mu', 'w0', 'w2', 'a0', 'a2', 'g2', 'k_k', 'k_a', 'r_k', 'ln_g', 'ln_b')
_LRU_NAMES = ('cw', 'cb', 'wa', 'ba', 'wx', 'bx', 'lam')


def _stack(x, m0):
    return jnp.concatenate([jnp.where(m0, x, 0.0), jnp.where(m0, 0.0, x)], axis=0)


def _mixer_prompt_kernel(*refs):
    n_w = len(_RW_NAMES) + len(_LRU_NAMES)
    prw_ref, puy_ref = refs[0], refs[1]
    w = dict(zip(_RW_NAMES + _LRU_NAMES, refs[2:2 + n_w]))
    yrw_ref, ylru_ref, sout_ref, hout_ref = refs[2 + n_w:6 + n_w]
    pbuf, ubuf, s_scr, h_scr = refs[6 + n_w:]
    C = CHUNK
    c = pl.program_id(1)

    @pl.when(c == 0)
    def _():
        pbuf[0:8, :] = jnp.zeros((8, N_SHIFT), F32)
        ubuf[0:8, :] = jnp.zeros((8, D_MODEL), F32)
        s_scr[...] = jnp.zeros_like(s_scr)
        h_scr[...] = jnp.zeros_like(h_scr)

    ones = _pair_ones()
    row = lax.broadcasted_iota(jnp.int32, (C, 1), 0)

    p = prw_ref[...]
    pbuf[8:8 + C, :] = p
    prev = pbuf[7:7 + C, :]
    ps = p + w['mu'][...] * (prev - p)
    r, k_mod, v, ld, a_vec, b_vec, g = _rwkv_elem(ps, w, ones)
    pbuf[0:8, :] = p[C - 8:C, :]

    tri = jnp.where(lax.broadcasted_iota(jnp.int32, (C, C), 0) >= lax.broadcasted_iota(jnp.int32, (C, C), 1),
                    1.0, 0.0).astype(BF16)
    l_hi = ld.astype(BF16)
    l_r1 = ld - l_hi.astype(F32)
    l_mid = l_r1.astype(BF16)
    l_lo = (l_r1 - l_mid.astype(F32)).astype(BF16)
    cum = _dot(tri, l_hi) + (_dot(tri, l_mid) + _dot(tri, l_lo))
    cum_last = cum[C - 1:C, :]
    e_neg = jnp.exp(-cum)
    at = a_vec * jnp.exp(cum - ld)
    rt = r * jnp.exp(cum)
    bt = b_vec * e_neg
    kt = k_mod * e_neg
    e_tail = jnp.exp(cum_last - cum)
    bh = b_vec * e_tail
    kh = k_mod * e_tail
    wc = jnp.exp(cum_last)
    bonus_dot = _segsum(r * k_mod * w['r_k'][...], ones)

    n = 2 * C
    ri = lax.broadcasted_iota(jnp.int32, (n, n), 0)
    ci = lax.broadcasted_iota(jnp.int32, (n, n), 1)
    low_strict = ri > ci
    low_incl = ri >= ci
    diag_blk = (ri // SUB) == (ci // SUB)
    eye = jnp.where(ri == ci, 1.0, 0.0)
    m0 = lax.broadcasted_iota(jnp.int32, (C, LANES), 1) < RW_HEAD

    for hp in range(N_PAIR):
        sl = slice(hp * LANES, (hp + 1) * LANES)
        as_ = _stack(at[:, sl], m0)
        rs_ = _stack(rt[:, sl], m0)
        bs_ = _stack(bt[:, sl], m0)
        ks_ = _stack(kt[:, sl], m0)
        vs_ = _stack(v[:, sl], m0)
        bhs = _stack(bh[:, sl], m0)
        khs = _stack(kh[:, sl], m0)
        s_bd = s_scr[hp]

        gram = _dot3(jnp.concatenate([as_, rs_], axis=0),
                     jnp.concatenate([bs_, ks_, s_bd], axis=0), _NT)
        n_ab = jnp.where(low_strict, gram[0:n, 0:n], 0.0)
        a_ak = jnp.where(low_strict, gram[0:n, n:2 * n], 0.0)
        a_s = gram[0:n, 2 * n:3 * n]
        r_b = jnp.where(low_incl, gram[n:2 * n, 0:n], 0.0)
        r_k = jnp.where(low_incl, gram[n:2 * n, n:2 * n], 0.0)
        r_s = gram[n:2 * n, 2 * n:3 * n]

        n_d = jnp.where(diag_blk, n_ab, 0.0)
        n_o = n_ab - n_d
        d_inv = eye + n_d
        q = n_d
        for _ in range(3):
            q = _dot3(q, q)
            d_inv = d_inv + _dot3(d_inv, q)
        m1 = _dot3(d_inv, n_o)
        m2 = _dot3(m1, m1)
        t_m = eye + m1 + m2 + _dot3(m1, m2)

        rhs_u = a_s + _dot3(a_ak, vs_)
        u = _dot3(t_m, _dot3(d_inv, rhs_u))
        uv = jnp.concatenate([u, vs_], axis=0)
        y_s = r_s + _dot3(jnp.concatenate([r_b, r_k], axis=1), uv)
        y = y_s[0:C] + y_s[C:n]
        s_new = s_bd * wc[:, sl] + _dot3(uv, jnp.concatenate([bhs, khs], axis=0), _TN)
        s_scr[hp] = s_new

        mu = _dot2x(y, ones) * (1.0 / RW_HEAD)
        yc = y - mu
        var = _dot2x(yc * yc, ones) * (1.0 / RW_HEAD)
        yn = yc * lax.rsqrt(var + GN_EPS) * w['ln_g'][:, sl] + w['ln_b'][:, sl]
        yrw_ref[:, sl] = (yn + bonus_dot[:, sl] * v[:, sl]) * g[:, sl]

    u_in = puy_ref[:, 0:D_MODEL]
    ubuf[8:8 + C, :] = u_in
    cw = w['cw']
    xc = (w['cb'][...] + ubuf[5:5 + C, :] * cw[0:1, :] + ubuf[6:6 + C, :] * cw[1:2, :]
          + ubuf[7:7 + C, :] * cw[2:3, :] + u_in * cw[3:4, :])
    ubuf[0:8, :] = u_in[C - 8:C, :]
    a_l, b_l, gel = _lru_elem(xc, puy_ref[:, D_MODEL:2 * D_MODEL], w)
    d = 1
    while d < C:
        keep = row >= d
        a_sh = jnp.where(keep, pltpu.roll(a_l, d, 0), 1.0)
        b_sh = jnp.where(keep, pltpu.roll(b_l, d, 0), 0.0)
        b_l = a_l * b_sh + b_l
        a_l = a_l * a_sh
        d *= 2
    h = a_l * h_scr[0:1, :] + b_l
    h_scr[0:1, :] = h[C - 1:C, :]
    ylru_ref[...] = h * gel

    @pl.when(c == pl.num_programs(1) - 1)
    def _():
        hout_ref[0] = h[C - 1:C, :]
        for hp in range(N_PAIR):
            s_bd = s_scr[hp]
            sout_ref[0, 2 * hp] = s_bd[0:RW_HEAD, 0:RW_HEAD]
            sout_ref[0, 2 * hp + 1] = s_bd[RW_HEAD:LANES, RW_HEAD:LANES]


def _mixer_prompt(p_rw, p_uy, weights, batch, seq):
    C = CHUNK
    nc = seq // C
    row_map = lambda b, c: (b * nc + c, 0)
    w_specs = [pl.BlockSpec(x.shape, (lambda b, c, nd=x.ndim: (0,) * nd)) for x in weights]
    return pl.pallas_call(
        _mixer_prompt_kernel,
        grid=(batch, nc),
        in_specs=[pl.BlockSpec((C, N_SHIFT), row_map), pl.BlockSpec((C, C_UY), row_map)] + w_specs,
        out_specs=[pl.BlockSpec((C, D_MODEL), row_map), pl.BlockSpec((C, D_MODEL), row_map),
                   pl.BlockSpec((1, RW_HEADS, RW_HEAD, RW_HEAD), lambda b, c: (b, 0, 0, 0)),
                   pl.BlockSpec((1, 1, D_MODEL), lambda b, c: (b, 0, 0))],
        out_shape=[jax.ShapeDtypeStruct((batch * seq, D_MODEL), F32),
                   jax.ShapeDtypeStruct((batch * seq, D_MODEL), F32),
                   jax.ShapeDtypeStruct((batch, RW_HEADS, RW_HEAD, RW_HEAD), F32),
                   jax.ShapeDtypeStruct((batch, 1, D_MODEL), F32)],
        scratch_shapes=[pltpu.VMEM((8 + C, N_SHIFT), F32), pltpu.VMEM((8 + C, D_MODEL), F32),
                        pltpu.VMEM((N_PAIR, LANES, LANES), F32), pltpu.VMEM((8, D_MODEL), F32)],
        compiler_params=_params(("parallel", "arbitrary")),
        name="mixer_prompt",
    )(p_rw, p_uy, *weights)


def _mixer_sample_prep_kernel(*refs):
    n_w = len(_RW_NAMES) + len(_LRU_NAMES)
    prw_ref, shift_ref, puy_ref, conv_ref, h0_ref = refs[0:5]
    w = dict(zip(_RW_NAMES + _LRU_NAMES, refs[5:5 + n_w]))
    rw_ref, ylru_ref, hnew_ref = refs[5 + n_w:]
    ones = _pair_ones()
    p = prw_ref[...]
    ps = p + w['mu'][...] * (shift_ref[...] - p)
    r, k_mod, v, ld, a_vec, b_vec, g = _rwkv_elem(ps, w, ones)
    rw_ref[0] = r
    rw_ref[1] = k_mod
    rw_ref[2] = v
    rw_ref[3] = jnp.exp(ld)
    rw_ref[4] = a_vec
    rw_ref[5] = b_vec
    rw_ref[6] = g
    u_in = puy_ref[:, 0:D_MODEL]
    cw = w['cw']
    xc = (w['cb'][...] + conv_ref[:, 0:D_MODEL] * cw[0:1, :] + conv_ref[:, D_MODEL:2 * D_MODEL] * cw[1:2, :]
          + conv_ref[:, 2 * D_MODEL:3 * D_MODEL] * cw[2:3, :] + u_in * cw[3:4, :])
    a_l, b_l, gel = _lru_elem(xc, puy_ref[:, D_MODEL:2 * D_MODEL], w)
    h = a_l * h0_ref[...] + b_l
    hnew_ref[...] = h
    ylru_ref[...] = h * gel


def _mixer_sample_prep(p_rw, shift, p_uy, conv, h0, weights):
    nb = p_rw.shape[0]
    ins = [p_rw, shift, p_uy, conv, h0] + list(weights)
    full = lambda x: pl.BlockSpec(x.shape, (lambda i, nd=x.ndim: (0,) * nd))
    return pl.pallas_call(
        _mixer_sample_prep_kernel,
        grid=(1,),
        in_specs=[full(x) for x in ins],
        out_specs=[pl.BlockSpec((7, nb, D_MODEL), lambda i: (0, 0, 0)),
                   pl.BlockSpec((nb, D_MODEL), lambda i: (0, 0)),
                   pl.BlockSpec((nb, D_MODEL), lambda i: (0, 0))],
        out_shape=[jax.ShapeDtypeStruct((7, nb, D_MODEL), F32),
                   jax.ShapeDtypeStruct((nb, D_MODEL), F32),
                   jax.ShapeDtypeStruct((nb, D_MODEL), F32)],
        compiler_params=_params(("arbitrary",)),
        name="mixer_sample_prep",
    )(*ins)


def _rwkv_step_kernel(s_ref, x_ref, rk_ref, lng_ref, lnb_ref, sout_ref, y_ref):
    s = s_ref[...]
    r, k, v, wd, a, b, g = (x_ref[i] for i in range(7))
    ri = lax.broadcasted_iota(jnp.int32, (RW_HEAD, RW_HEAD), 0)
    ci = lax.broadcasted_iota(jnp.int32, (RW_HEAD, RW_HEAD), 1)
    eye = jnp.where(ri == ci, 1.0, 0.0)
    sa = jnp.sum(s * a, axis=-1, keepdims=True)
    v_col = jnp.sum(eye * v, axis=-1, keepdims=True)
    s_new = s * wd + sa * b + v_col * k
    sout_ref[...] = s_new
    y_col = jnp.sum(s_new * r, axis=-1, keepdims=True)
    y = jnp.sum(eye * y_col, axis=-2, keepdims=True)
    mu = jnp.mean(y, -1, keepdims=True)
    yc = y - mu
    var = jnp.mean(yc * yc, -1, keepdims=True)
    yn = yc * lax.rsqrt(var + GN_EPS) * lng_ref[...] + lnb_ref[...]
    bonus = jnp.sum(r * k * rk_ref[...], -1, keepdims=True) * v
    y_ref[...] = (yn + bonus) * g


def _rwkv_step(state, rw7, r_k, ln_g, ln_b, nb):
    n = state.shape[0]
    hd = (RW_HEADS, 1, RW_HEAD)
    return pl.pallas_call(
        _rwkv_step_kernel,
        grid=(n // nb,),
        in_specs=[pl.BlockSpec((nb, RW_HEADS, RW_HEAD, RW_HEAD), lambda i: (i, 0, 0, 0)),
                  pl.BlockSpec((7, nb) + hd, lambda i: (0, i, 0, 0, 0)),
                  pl.BlockSpec(hd, lambda i: (0, 0, 0)), pl.BlockSpec(hd, lambda i: (0, 0, 0)),
                  pl.BlockSpec(hd, lambda i: (0, 0, 0))],
        out_specs=[pl.BlockSpec((nb, RW_HEADS, RW_HEAD, RW_HEAD), lambda i: (i, 0, 0, 0)),
                   pl.BlockSpec((nb,) + hd, lambda i: (i, 0, 0, 0))],
        out_shape=[jax.ShapeDtypeStruct(state.shape, F32), jax.ShapeDtypeStruct((n,) + hd, F32)],
        compiler_params=_params(("parallel",)),
        name="rwkv_step",
    )(state, rw7, r_k, ln_g, ln_b)


def _mm_kernel(x_ref, w_ref, o_ref):
    o_ref[...] = _dot(x_ref[...], w_ref[...])


def _mm(x, w, tm, tn, name):
    m, kd = x.shape
    n = w.shape[1]
    return pl.pallas_call(
        _mm_kernel,
        grid=(m // tm, n // tn),
        in_specs=[pl.BlockSpec((tm, kd), lambda i, j: (i, 0)), pl.BlockSpec((kd, tn), lambda i, j: (0, j))],
        out_specs=pl.BlockSpec((tm, tn), lambda i, j: (i, j)),
        out_shape=jax.ShapeDtypeStruct((m, n), F32),
        compiler_params=_params(("parallel", "parallel")),
        name=name,
    )(x, w)


def _post_kernel(x_ref, yrw_ref, ylru_ref, pg_ref, wrw_ref, wlru_ref, wmix_ref, gb_ref, g_ref, b_ref, o_ref):
    o_rw = _dot(yrw_ref[...], wrw_ref[...])
    o_lru = _dot(ylru_ref[...], wlru_ref[...])
    gates = _sigmoid(pg_ref[...] + gb_ref[...])
    mix = _dot(gates[:, 0:D_MODEL] * o_rw + gates[:, D_MODEL:2 * D_MODEL] * o_lru, wmix_ref[...])
    o_ref[...] = _layer_norm(ALPHA * x_ref[...] + mix, g_ref[...], b_ref[...])


def _post(x, y_rw, y_lru, p_gate, w_rw, w_lru, w_mix, gate_b, g, b, tm):
    m = x.shape[0]
    rows = lambda width: pl.BlockSpec((tm, width), lambda i: (i, 0))
    const = lambda a: pl.BlockSpec(a.shape, lambda i: (0, 0))
    return pl.pallas_call(
        _post_kernel,
        grid=(m // tm,),
        in_specs=[rows(D_MODEL), rows(D_MODEL), rows(D_MODEL), rows(2 * D_MODEL),
                  const(w_rw), const(w_lru), const(w_mix), const(gate_b), const(g), const(b)],
        out_specs=rows(D_MODEL),
        out_shape=jax.ShapeDtypeStruct((m, D_MODEL), F32),
        compiler_params=_params(("parallel",)),
        name="post",
    )(x, y_rw, y_lru, p_gate, w_rw, w_lru, w_mix, gate_b, g, b)


def _attn_kernel(q_ref, k_ref, v_ref, o_ref, *, tq):
    q = q_ref[0]
    if tq < 8:
        q = jnp.broadcast_to(q[0:1, :], (8, D_MODEL))
    for h in range(XA_HEADS):
        sl = slice(h * XA_HEAD, (h + 1) * XA_HEAD)
        s = _dot(q[:, sl], k_ref[0, :, sl], _NT) * (XA_HEAD ** -0.5)
        e = jnp.exp(s - jnp.max(s, -1, keepdims=True))
        prob = e / jnp.sum(e, -1, keepdims=True)
        o = _dot(prob, v_ref[0, :, sl])
        o_ref[0, :, sl] = o[0:tq, :]


def _attn(q, k, v, tq):
    b, t, _ = q.shape
    return pl.pallas_call(
        functools.partial(_attn_kernel, tq=tq),
        grid=(b, t // tq),
        in_specs=[pl.BlockSpec((1, tq, D_MODEL), lambda i, j: (i, j, 0)),
                  pl.BlockSpec((1, N_MEM, D_MODEL), lambda i, j: (i, 0, 0)),
                  pl.BlockSpec((1, N_MEM, D_MODEL), lambda i, j: (i, 0, 0))],
        out_specs=pl.BlockSpec((1, tq, D_MODEL), lambda i, j: (i, j, 0)),
        out_shape=jax.ShapeDtypeStruct(q.shape, F32),
        compiler_params=_params(("parallel", "parallel")),
        name="attn",
    )(q, k, v)


def _mm_ln_kernel(a_ref, w_ref, x_ref, g_ref, b_ref, o_ref):
    o_ref[...] = _layer_norm(ALPHA * x_ref[...] + _dot(a_ref[...], w_ref[...]), g_ref[...], b_ref[...])


def _mm_ln(a, w, x, g, b, tm):
    m = a.shape[0]
    rows = pl.BlockSpec((tm, D_MODEL), lambda i: (i, 0))
    const = lambda z: pl.BlockSpec(z.shape, lambda i: (0, 0))
    return pl.pallas_call(
        _mm_ln_kernel,
        grid=(m // tm,),
        in_specs=[rows, const(w), rows, const(g), const(b)],
        out_specs=rows,
        out_shape=jax.ShapeDtypeStruct((m, D_MODEL), F32),
        compiler_params=_params(("parallel",)),
        name="mm_ln",
    )(a, w, x, g, b)


def _mlp_kernel(x_ref, up_ref, down_ref, g_ref, b_ref, o_ref, acc_ref):
    f = pl.program_id(1)
    x = x_ref[...]
    hdn = jnp.square(jnp.maximum(_dot(x, up_ref[...]), 0.0))
    part = _dot(hdn, down_ref[...])

    @pl.when(f == 0)
    def _():
        acc_ref[...] = part

    @pl.when(f > 0)
    def _():
        acc_ref[...] += part

    @pl.when(f == pl.num_programs(1) - 1)
    def _():
        o_ref[...] = _layer_norm(ALPHA * x + acc_ref[...], g_ref[...], b_ref[...])


def _mlp(x, up, down, g, b, tm, tf):
    m = x.shape[0]
    return pl.pallas_call(
        _mlp_kernel,
        grid=(m // tm, D_FF // tf),
        in_specs=[pl.BlockSpec((tm, D_MODEL), lambda i, f: (i, 0)),
                  pl.BlockSpec((D_MODEL, tf), lambda i, f: (0, f)),
                  pl.BlockSpec((tf, D_MODEL), lambda i, f: (f, 0)),
                  pl.BlockSpec((1, D_MODEL), lambda i, f: (0, 0)),
                  pl.BlockSpec((1, D_MODEL), lambda i, f: (0, 0))],
        out_specs=pl.BlockSpec((tm, D_MODEL), lambda i, f: (i, 0)),
        out_shape=jax.ShapeDtypeStruct((m, D_MODEL), F32),
        scratch_shapes=[pltpu.VMEM((tm, D_MODEL), F32)],
        compiler_params=_params(("parallel", "arbitrary")),
        name="mlp",
    )(x, up, down, g, b)


def _row(x):
    return x.reshape(1, -1)


def _pair_blocks(wb):
    z = jnp.zeros((N_PAIR, RW_HEAD, RW_HEAD), wb.dtype)
    even, odd = wb[0::2], wb[1::2]
    top = jnp.concatenate([even, z], axis=2)
    bot = jnp.concatenate([z, odd], axis=2)
    return jnp.concatenate([top, bot], axis=1).astype(BF16)


def _layer_weights(l, W):
    z64 = jnp.zeros((64, D_MODEL), F32)
    rw = [_row(W['mu_shift'][l]), _row(W['rw_w0'][l]),
          jnp.concatenate([W['rw_w2'][l], z64], 0).astype(BF16), _row(W['rw_a0'][l]),
          jnp.concatenate([z64, W['rw_a2'][l]], 0).astype(BF16), W['rw_g2'][l].astype(BF16),
          _row(W['rw_k_k'][l]), _row(W['rw_k_a'][l]), _row(W['rw_r_k'][l]),
          _row(W['rw_lnx_g'][l]), _row(W['rw_lnx_b'][l])]
    lru = [W['lru_conv_w'][l], _row(W['lru_conv_b'][l]), _pair_blocks(W['lru_wa'][l]), _row(W['lru_ba'][l]),
           _pair_blocks(W['lru_wx'][l]), _row(W['lru_bx'][l]), _row(W['lru_lambda'][l])]
    return rw + lru


def _trunk(x, mem_k, mem_v, states, W, tm, tq):
    batch, seq, _ = x.shape
    m = batch * seq
    x2 = x.reshape(m, D_MODEL)
    o_rw_st, o_shift_st, o_h_st, o_conv_st = [], [], [], []
    for l in range(DEPTH):
        w_in = W['w_in'][l]
        mixw = _layer_weights(l, W)
        p_rw = _mm(x2, w_in[:, 0:N_SHIFT].astype(BF16), tm, N_SHIFT // 2, "proj_rw")
        p_uy = _mm(x2, w_in[:, N_SHIFT:N_SHIFT + C_UY].astype(BF16), tm, 1024, "proj_uy")
        p_gate = _mm(x2, w_in[:, N_SHIFT + C_UY:].astype(BF16), tm, 1024, "proj_gate")
        if states is None:
            y_rw, y_lru, s_new, h_new = _mixer_prompt(p_rw, p_uy, mixw, batch, seq)
            h_new = h_new.reshape(batch, D_MODEL)
            conv_new = p_uy.reshape(batch, seq, C_UY)[:, seq - (CONV_W - 1):, 0:D_MODEL]
        else:
            s_rw, s_shift, s_h, s_conv = states
            rw7, y_lru, h_new = _mixer_sample_prep(
                p_rw, s_shift[l], p_uy, s_conv[l].reshape(batch, (CONV_W - 1) * D_MODEL), s_h[l], mixw)
            hd = (RW_HEADS, 1, RW_HEAD)
            s_new, y_rw = _rwkv_step(s_rw[l], rw7.reshape((7, batch) + hd), W['rw_r_k'][l].reshape(hd),
                                     W['rw_lnx_g'][l].reshape(hd), W['rw_lnx_b'][l].reshape(hd), 8)
            y_rw = y_rw.reshape(batch, D_MODEL)
            conv_new = jnp.concatenate([s_conv[l][:, 1:], p_uy[:, None, 0:D_MODEL]], axis=1)
        x2 = _post(x2, y_rw, y_lru, p_gate, W['rw_proj'][l].astype(BF16), W['lru_proj'][l].astype(BF16),
                   W['w_out_mix'][l].astype(BF16), W['mix_gate_b'][l].reshape(1, 2 * D_MODEL),
                   _row(W['ln1_g'][l]), _row(W['ln1_b'][l]), min(tm, 256))
        q = _mm(x2, W['xa_wq'][l].astype(BF16), tm, 1024, "proj_q")
        o = _attn(q.reshape(batch, seq, D_MODEL), mem_k[l], mem_v[l], tq)
        x2 = _mm_ln(o.reshape(m, D_MODEL), W['xa_wo'][l].astype(BF16), x2,
                    _row(W['ln2_g'][l]), _row(W['ln2_b'][l]), min(tm, 512))
        x2 = _mlp(x2, W['mlp_up'][l].astype(BF16), W['mlp_down'][l].astype(BF16),
                  _row(W['ln3_g'][l]), _row(W['ln3_b'][l]), tm, 1024)
        o_rw_st.append(s_new)
        o_shift_st.append(p_rw.reshape(batch, seq, N_SHIFT)[:, seq - 1])
        o_h_st.append(h_new)
        o_conv_st.append(conv_new)
    return x2.reshape(batch, seq, D_MODEL), (jnp.stack(o_rw_st, 0), jnp.stack(o_shift_st, 0),
                                             jnp.stack(o_h_st, 0), jnp.stack(o_conv_st, 0))


def kernel(x_prompt, x_sample, mem_prompt, cache_mem_k, cache_mem_v, state_rwkv, state_rwkv_shift, state_lru_h, state_lru_conv, w_in, mu_shift, rw_w0, rw_w2, rw_a0, rw_a2, rw_g2, rw_k_k, rw_k_a, rw_r_k, rw_lnx_g, rw_lnx_b, rw_proj, lru_conv_w, lru_conv_b, lru_wa, lru_ba, lru_wx, lru_bx, lru_lambda, lru_proj, mix_gate_b, w_out_mix, ln1_g, ln1_b, xa_wq, xa_wk, xa_wv, xa_wo, ln2_g, ln2_b, mlp_up, mlp_down, ln3_g, ln3_b):
    W = dict(w_in=w_in, mu_shift=mu_shift, rw_w0=rw_w0, rw_w2=rw_w2, rw_a0=rw_a0, rw_a2=rw_a2,
             rw_g2=rw_g2, rw_k_k=rw_k_k, rw_k_a=rw_k_a, rw_r_k=rw_r_k, rw_lnx_g=rw_lnx_g,
             rw_lnx_b=rw_lnx_b, rw_proj=rw_proj, lru_conv_w=lru_conv_w, lru_conv_b=lru_conv_b,
             lru_wa=lru_wa, lru_ba=lru_ba, lru_wx=lru_wx, lru_bx=lru_bx, lru_lambda=lru_lambda,
             lru_proj=lru_proj, mix_gate_b=mix_gate_b, w_out_mix=w_out_mix, ln1_g=ln1_g, ln1_b=ln1_b,
             xa_wq=xa_wq, xa_wo=xa_wo, ln2_g=ln2_g, ln2_b=ln2_b, mlp_up=mlp_up, mlp_down=mlp_down,
             ln3_g=ln3_g, ln3_b=ln3_b)
    bp, seq, _ = x_prompt.shape
    bs = x_sample.shape[0]
    mem2 = mem_prompt.reshape(bp * N_MEM, D_MODEL)
    tmem = min(512, bp * N_MEM)
    mem_k_p = jnp.stack([_mm(mem2, xa_wk[l].astype(BF16), tmem, 1024, "mem_k") for l in range(DEPTH)], 0)
    mem_v_p = jnp.stack([_mm(mem2, xa_wv[l].astype(BF16), tmem, 1024, "mem_v") for l in range(DEPTH)], 0)
    mem_k_p = mem_k_p.reshape(DEPTH, bp, N_MEM, D_MODEL)
    mem_v_p = mem_v_p.reshape(DEPTH, bp, N_MEM, D_MODEL)
    tm_p = min(512, bp * seq)
    y_prompt, (p_rw, p_shift, p_h, p_conv) = _trunk(x_prompt, mem_k_p, mem_v_p, None, W, tm_p, min(512, seq))
    y_sample, (s_rw, s_shift, s_h, s_conv) = _trunk(
        x_sample, cache_mem_k.reshape(DEPTH, bs, N_MEM, D_MODEL), cache_mem_v.reshape(DEPTH, bs, N_MEM, D_MODEL),
        (state_rwkv, state_rwkv_shift, state_lru_h, state_lru_conv), W, bs, 1)
    kv_shape = (DEPTH, bp, N_MEM, XA_HEADS, XA_HEAD)
    return (y_prompt, y_sample, p_rw, p_shift, p_h, p_conv, mem_k_p.reshape(kv_shape), mem_v_p.reshape(kv_shape),
            s_rw, s_shift, s_h, s_conv)
```

```python
import functools

import jax
import jax.numpy as jnp
from jax import lax
from jax.experimental import pallas as pl
from jax.experimental.pallas import tpu as pltpu

F32 = jnp.float32
BF16 = jnp.bfloat16

D_MODEL = 1024
DEPTH = 2
RW_HEAD = 64
RW_HEADS = 16
N_PAIR = 8
LANES = 128
R_GATE = 128
GN_EPS = 64e-5
LRU_C = 8.0
CONV_W = 4
N_MEM = 256
XA_HEADS = 4
XA_HEAD = 256
D_FF = 4096
ALPHA = (2 * DEPTH) ** 0.25
LN_EPS = 1e-5
N_SHIFT = 3328
C_UY = 2048
CHUNK = 64
SUB = 16
VMEM_LIMIT = 56 * 1024 * 1024

_NT = (((1,), (1,)), ((), ()))
_TN = (((0,), (0,)), ((), ()))


def _params(sem):
    return pltpu.CompilerParams(dimension_semantics=sem, vmem_limit_bytes=VMEM_LIMIT)


def _sigmoid(x):
    return 1.0 / (1.0 + jnp.exp(-x))


def _softplus(x):
    return jnp.maximum(x, 0.0) + jnp.log(1.0 + jnp.exp(-jnp.abs(x)))


def _gelu_tanh(x):
    return 0.5 * x * (1.0 + jnp.tanh(0.7978845608028654 * (x + 0.044715 * (x * x * x))))


def _layer_norm(z, g, b):
    mu = jnp.mean(z, -1, keepdims=True)
    zc = z - mu
    var = jnp.mean(zc * zc, -1, keepdims=True)
    return zc * lax.rsqrt(var + LN_EPS) * g + b


def _dot(a, b, dims=None):
    a = a.astype(BF16)
    b = b.astype(BF16)
    if dims is None:
        return jnp.dot(a, b, preferred_element_type=F32)
    return lax.dot_general(a, b, dims, preferred_element_type=F32)


def _split(a):
    hi = a.astype(BF16)
    lo = (a - hi.astype(F32)).astype(BF16)
    return hi, lo


def _dot3(a, b, dims=None):
    ah, al = _split(a)
    bh, bl = _split(b)
    return _dot(ah, bh, dims) + (_dot(ah, bl, dims) + _dot(al, bh, dims))


def _dot2x(a, b_exact):
    ah, al = _split(a)
    return _dot(ah, b_exact) + _dot(al, b_exact)


def _pair_ones():
    r = lax.broadcasted_iota(jnp.int32, (LANES, LANES), 0) // RW_HEAD
    c = lax.broadcasted_iota(jnp.int32, (LANES, LANES), 1) // RW_HEAD
    return jnp.where(r == c, 1.0, 0.0).astype(BF16)


def _segsum(x, ones):
    cols = [_dot2x(x[:, p * LANES:(p + 1) * LANES], ones) for p in range(x.shape[1] // LANES)]
    return jnp.concatenate(cols, axis=1) if len(cols) > 1 else cols[0]


def _blockdot(x, w_ref):
    cols = [_dot(x[:, p * LANES:(p + 1) * LANES], w_ref[p]) for p in range(N_PAIR)]
    return jnp.concatenate(cols, axis=1)


def _rwkv_elem(ps, w, ones):
    r = ps[:, 0:1024]
    k = ps[:, 1024:2048]
    v = ps[:, 2048:3072]
    slab = ps[:, 3072:3200]
    xg = ps[:, 3200:3328]
    wlog = -_softplus(-(w['w0'][...] + _dot(jnp.tanh(slab), w['w2'][...]))) - 0.5
    ld = -jnp.exp(wlog)
    a_sig = _sigmoid(w['a0'][...] + _dot(slab, w['a2'][...]))
    g = _dot(_sigmoid(xg), w['g2'][...])
    kk = k * w['k_k'][...]
    kk = kk * lax.rsqrt(jnp.maximum(_segsum(kk * kk, ones), 1e-24))
    k_mod = k * (1.0 + (a_sig - 1.0) * w['k_a'][...])
    return r, k_mod, v, ld, -kk, kk * a_sig, g


def _lru_elem(xc, y_in, w):
    gr = _sigmoid(_blockdot(xc, w['wa']) + w['ba'][...])
    gi = _sigmoid(_blockdot(xc, w['wx']) + w['bx'][...])
    log_a = -LRU_C * _softplus(-w['lam'][...]) * gr
    a = jnp.exp(log_a)
    bterm = jnp.sqrt(1.0 - jnp.exp(2.0 * log_a)) * (gi * xc)
    return a, bterm, _gelu_tanh(y_in)


_RW_NAMES = ('mu', 'w0', 'w2', 'a0', 'a2', 'g2', 'k_k', 'k_a', 'r_k', 'ln_g', 'ln_b')
_LRU_NAMES = ('cw', 'cb', 'wa', 'ba', 'wx', 'bx', 'lam')


def _stack(x, m0):
    return jnp.concatenate([jnp.where(m0, x, 0.0), jnp.where(m0, 0.0, x)], axis=0)


def _mixer_prompt_kernel(*refs):
    n_w = len(_RW_NAMES) + len(_LRU_NAMES)
    prw_ref, puy_ref = refs[0], refs[1]
    w = dict(zip(_RW_NAMES + _LRU_NAMES, refs[2:2 + n_w]))
    yrw_ref, ylru_ref, sout_ref, hout_ref = refs[2 + n_w:6 + n_w]
    pbuf, ubuf, s_scr, h_scr = refs[6 + n_w:]
    C = CHUNK
    c = pl.program_id(1)

    @pl.when(c == 0)
    def _():
        pbuf[0:8, :] = jnp.zeros((8, N_SHIFT), F32)
        ubuf[0:8, :] = jnp.zeros((8, D_MODEL), F32)
        s_scr[...] = jnp.zeros_like(s_scr)
        h_scr[...] = jnp.zeros_like(h_scr)

    ones = _pair_ones()
    row = lax.broadcasted_iota(jnp.int32, (C, 1), 0)

    p = prw_ref[...]
    pbuf[8:8 + C, :] = p
    prev = pbuf[7:7 + C, :]
    ps = p + w['mu'][...] * (prev - p)
    r, k_mod, v, ld, a_vec, b_vec, g = _rwkv_elem(ps, w, ones)
    pbuf[0:8, :] = p[C - 8:C, :]

    tri = jnp.where(lax.broadcasted_iota(jnp.int32, (C, C), 0) >= lax.broadcasted_iota(jnp.int32, (C, C), 1),
                    1.0, 0.0).astype(BF16)
    l_hi = ld.astype(BF16)
    l_r1 = ld - l_hi.astype(F32)
    l_mid = l_r1.astype(BF16)
    l_lo = (l_r1 - l_mid.astype(F32)).astype(BF16)
    cum = _dot(tri, l_hi) + (_dot(tri, l_mid) + _dot(tri, l_lo))
    cum_last = cum[C - 1:C, :]
    e_neg = jnp.exp(-cum)
    at = a_vec * jnp.exp(cum - ld)
    rt = r * jnp.exp(cum)
    bt = b_vec * e_neg
    kt = k_mod * e_neg
    e_tail = jnp.exp(cum_last - cum)
    bh = b_vec * e_tail
    kh = k_mod * e_tail
    wc = jnp.exp(cum_last)
    bonus_dot = _segsum(r * k_mod * w['r_k'][...], ones)

    n = 2 * C
    ri = lax.broadcasted_iota(jnp.int32, (n, n), 0)
    ci = lax.broadcasted_iota(jnp.int32, (n, n), 1)
    low_strict = ri > ci
    low_incl = ri >= ci
    diag_blk = (ri // SUB) == (ci // SUB)
    eye = jnp.where(ri == ci, 1.0, 0.0)
    m0 = lax.broadcasted_iota(jnp.int32, (C, LANES), 1) < RW_HEAD

    pairs = range(N_PAIR)
    sls = [slice(hp * LANES, (hp + 1) * LANES) for hp in pairs]
    vs_ = [_stack(v[:, sl], m0) for sl in sls]
    s_bd = [s_scr[hp] for hp in pairs]
    gram = [_dot3(jnp.concatenate([_stack(at[:, sl], m0), _stack(rt[:, sl], m0)], axis=0),
                  jnp.concatenate([_stack(bt[:, sl], m0), _stack(kt[:, sl], m0), s_bd[hp]], axis=0), _NT)
            for hp, sl in zip(pairs, sls)]
    n_ab = [jnp.where(low_strict, gm[0:n, 0:n], 0.0) for gm in gram]
    a_ak = [jnp.where(low_strict, gm[0:n, n:2 * n], 0.0) for gm in gram]
    r_bk = [jnp.concatenate([jnp.where(low_incl, gm[n:2 * n, 0:n], 0.0),
                             jnp.where(low_incl, gm[n:2 * n, n:2 * n], 0.0)], axis=1) for gm in gram]
    rhs_u = [gm[0:n, 2 * n:3 * n] + _dot3(ak, vv) for gm, ak, vv in zip(gram, a_ak, vs_)]

    n_d = [jnp.where(diag_blk, x, 0.0) for x in n_ab]
    n_o = [x - d for x, d in zip(n_ab, n_d)]
    d_inv = [eye + d for d in n_d]
    q = n_d
    for _ in range(3):
        q = [_dot3(x, x) for x in q]
        d_inv = [d + _dot3(d, x) for d, x in zip(d_inv, q)]
    m1 = [_dot3(d, o) for d, o in zip(d_inv, n_o)]
    m2 = [_dot3(x, x) for x in m1]
    t_m = [eye + x1 + x2 + _dot3(x1, x2) for x1, x2 in zip(m1, m2)]
    u = [_dot3(d, x) for d, x in zip(d_inv, rhs_u)]
    u = [_dot3(t, x) for t, x in zip(t_m, u)]
    uv = [jnp.concatenate([x, vv], axis=0) for x, vv in zip(u, vs_)]
    y_s = [gm[n:2 * n, 2 * n:3 * n] + _dot3(rb, x) for gm, rb, x in zip(gram, r_bk, uv)]
    s_new = [s_bd[hp] * wc[:, sl]
             + _dot3(uv[hp], jnp.concatenate([_stack(bh[:, sl], m0), _stack(kh[:, sl], m0)], axis=0), _TN)
             for hp, sl in zip(pairs, sls)]
    for hp in pairs:
        s_scr[hp] = s_new[hp]

    y = [x[0:C] + x[C:n] for x in y_s]
    mu = [_dot2x(x, ones) * (1.0 / RW_HEAD) for x in y]
    yc = [x - m for x, m in zip(y, mu)]
    var = [_dot2x(x * x, ones) * (1.0 / RW_HEAD) for x in yc]
    for hp, sl in zip(pairs, sls):
        yn = yc[hp] * lax.rsqrt(var[hp] + GN_EPS) * w['ln_g'][:, sl] + w['ln_b'][:, sl]
        yrw_ref[:, sl] = (yn + bonus_dot[:, sl] * v[:, sl]) * g[:, sl]

    u_in = puy_ref[:, 0:D_MODEL]
    ubuf[8:8 + C, :] = u_in
    cw = w['cw']
    xc = (w['cb'][...] + ubuf[5:5 + C, :] * cw[0:1, :] + ubuf[6:6 + C, :] * cw[1:2, :]
          + ubuf[7:7 + C, :] * cw[2:3, :] + u_in * cw[3:4, :])
    ubuf[0:8, :] = u_in[C - 8:C, :]
    a_l, b_l, gel = _lru_elem(xc, puy_ref[:, D_MODEL:2 * D_MODEL], w)
    d = 1
    while d < C:
        keep = row >= d
        a_sh = jnp.where(keep, pltpu.roll(a_l, d, 0), 1.0)
        b_sh = jnp.where(keep, pltpu.roll(b_l, d, 0), 0.0)
        b_l = a_l * b_sh + b_l
        a_l = a_l * a_sh
        d *= 2
    h = a_l * h_scr[0:1, :] + b_l
    h_scr[0:1, :] = h[C - 1:C, :]
    ylru_ref[...] = h * gel

    @pl.when(c == pl.num_programs(1) - 1)
    def _():
        hout_ref[0] = h[C - 1:C, :]
        for hp in range(N_PAIR):
            s_bd = s_scr[hp]
            sout_ref[0, 2 * hp] = s_bd[0:RW_HEAD, 0:RW_HEAD]
            sout_ref[0, 2 * hp + 1] = s_bd[RW_HEAD:LANES, RW_HEAD:LANES]


def _mixer_prompt(p_rw, p_uy, weights, batch, seq):
    C = CHUNK
    nc = seq // C
    row_map = lambda b, c: (b * nc + c, 0)
    w_specs = [pl.BlockSpec(x.shape, (lambda b, c, nd=x.ndim: (0,) * nd)) for x in weights]
    return pl.pallas_call(
        _mixer_prompt_kernel,
        grid=(batch, nc),
        in_specs=[pl.BlockSpec((C, N_SHIFT), row_map), pl.BlockSpec((C, C_UY), row_map)] + w_specs,
        out_specs=[pl.BlockSpec((C, D_MODEL), row_map), pl.BlockSpec((C, D_MODEL), row_map),
                   pl.BlockSpec((1, RW_HEADS, RW_HEAD, RW_HEAD), lambda b, c: (b, 0, 0, 0)),
                   pl.BlockSpec((1, 1, D_MODEL), lambda b, c: (b, 0, 0))],
        out_shape=[jax.ShapeDtypeStruct((batch * seq, D_MODEL), F32),
                   jax.ShapeDtypeStruct((batch * seq, D_MODEL), F32),
                   jax.ShapeDtypeStruct((batch, RW_HEADS, RW_HEAD, RW_HEAD), F32),
                   jax.ShapeDtypeStruct((batch, 1, D_MODEL), F32)],
        scratch_shapes=[pltpu.VMEM((8 + C, N_SHIFT), F32), pltpu.VMEM((8 + C, D_MODEL), F32),
                        pltpu.VMEM((N_PAIR, LANES, LANES), F32), pltpu.VMEM((8, D_MODEL), F32)],
        compiler_params=_params(("parallel", "arbitrary")),
        name="mixer_prompt",
    )(p_rw, p_uy, *weights)


def _mixer_sample_prep_kernel(*refs):
    n_w = len(_RW_NAMES) + len(_LRU_NAMES)
    prw_ref, shift_ref, puy_ref, conv_ref, h0_ref = refs[0:5]
    w = dict(zip(_RW_NAMES + _LRU_NAMES, refs[5:5 + n_w]))
    rw_ref, ylru_ref, hnew_ref = refs[5 + n_w:]
    ones = _pair_ones()
    p = prw_ref[...]
    ps = p + w['mu'][...] * (shift_ref[...] - p)
    r, k_mod, v, ld, a_vec, b_vec, g = _rwkv_elem(ps, w, ones)
    rw_ref[0] = r
    rw_ref[1] = k_mod
    rw_ref[2] = v
    rw_ref[3] = jnp.exp(ld)
    rw_ref[4] = a_vec
    rw_ref[5] = b_vec
    rw_ref[6] = g
    u_in = puy_ref[:, 0:D_MODEL]
    cw = w['cw']
    xc = (w['cb'][...] + conv_ref[:, 0:D_MODEL] * cw[0:1, :] + conv_ref[:, D_MODEL:2 * D_MODEL] * cw[1:2, :]
          + conv_ref[:, 2 * D_MODEL:3 * D_MODEL] * cw[2:3, :] + u_in * cw[3:4, :])
    a_l, b_l, gel = _lru_elem(xc, puy_ref[:, D_MODEL:2 * D_MODEL], w)
    h = a_l * h0_ref[...] + b_l
    hnew_ref[...] = h
    ylru_ref[...] = h * gel


def _mixer_sample_prep(p_rw, shift, p_uy, conv, h0, weights):
    nb = p_rw.shape[0]
    ins = [p_rw, shift, p_uy, conv, h0] + list(weights)
    full = lambda x: pl.BlockSpec(x.shape, (lambda i, nd=x.ndim: (0,) * nd))
    return pl.pallas_call(
        _mixer_sample_prep_kernel,
        grid=(1,),
        in_specs=[full(x) for x in ins],
        out_specs=[pl.BlockSpec((7, nb, D_MODEL), lambda i: (0, 0, 0)),
                   pl.BlockSpec((nb, D_MODEL), lambda i: (0, 0)),
                   pl.BlockSpec((nb, D_MODEL), lambda i: (0, 0))],
        out_shape=[jax.ShapeDtypeStruct((7, nb, D_MODEL), F32),
                   jax.ShapeDtypeStruct((nb, D_MODEL), F32),
                   jax.ShapeDtypeStruct((nb, D_MODEL), F32)],
        compiler_params=_params(("arbitrary",)),
        name="mixer_sample_prep",
    )(*ins)


def _rwkv_step_kernel(s_ref, x_ref, rk_ref, lng_ref, lnb_ref, sout_ref, y_ref):
    s = s_ref[...]
    r, k, v, wd, a, b, g = (x_ref[i] for i in range(7))
    ri = lax.broadcasted_iota(jnp.int32, (RW_HEAD, RW_HEAD), 0)
    ci = lax.broadcasted_iota(jnp.int32, (RW_HEAD, RW_HEAD), 1)
    eye = jnp.where(ri == ci, 1.0, 0.0)
    sa = jnp.sum(s * a, axis=-1, keepdims=True)
    v_col = jnp.sum(eye * v, axis=-1, keepdims=True)
    s_new = s * wd + sa * b + v_col * k
    sout_ref[...] = s_new
    y_col = jnp.sum(s_new * r, axis=-1, keepdims=True)
    y = jnp.sum(eye * y_col, axis=-2, keepdims=True)
    mu = jnp.mean(y, -1, keepdims=True)
    yc = y - mu
    var = jnp.mean(yc * yc, -1, keepdims=True)
    yn = yc * lax.rsqrt(var + GN_EPS) * lng_ref[...] + lnb_ref[...]
    bonus = jnp.sum(r * k * rk_ref[...], -1, keepdims=True) * v
    y_ref[...] = (yn + bonus) * g


def _rwkv_step(state, l, rw7, r_k, ln_g, ln_b, nb):
    n = state.shape[1]
    hd = (RW_HEADS, 1, RW_HEAD)
    return pl.pallas_call(
        _rwkv_step_kernel,
        grid=(n // nb,),
        in_specs=[pl.BlockSpec((None, nb, RW_HEADS, RW_HEAD, RW_HEAD), lambda i: (l, i, 0, 0, 0)),
                  pl.BlockSpec((7, nb) + hd, lambda i: (0, i, 0, 0, 0)),
                  pl.BlockSpec(hd, lambda i: (0, 0, 0)), pl.BlockSpec(hd, lambda i: (0, 0, 0)),
                  pl.BlockSpec(hd, lambda i: (0, 0, 0))],
        out_specs=[pl.BlockSpec((nb, RW_HEADS, RW_HEAD, RW_HEAD), lambda i: (i, 0, 0, 0)),
                   pl.BlockSpec((nb,) + hd, lambda i: (i, 0, 0, 0))],
        out_shape=[jax.ShapeDtypeStruct(state.shape[1:], F32), jax.ShapeDtypeStruct((n,) + hd, F32)],
        compiler_params=_params(("parallel",)),
        name="rwkv_step",
    )(state, rw7, r_k, ln_g, ln_b)


def _mm_kernel(x_ref, w_ref, o_ref):
    o_ref[...] = _dot(x_ref[...], w_ref[...])


def _mm(x, w, l, tm, tn, name):
    m, kd = x.shape
    n = w.shape[2]
    return pl.pallas_call(
        _mm_kernel,
        grid=(m // tm, n // tn),
        in_specs=[pl.BlockSpec((tm, kd), lambda i, j: (i, 0)),
                  pl.BlockSpec((None, kd, tn), lambda i, j: (l, 0, j))],
        out_specs=pl.BlockSpec((tm, tn), lambda i, j: (i, j)),
        out_shape=jax.ShapeDtypeStruct((m, n), F32),
        compiler_params=_params(("parallel", "parallel")),
        name=name,
    )(x, w)


def _post_kernel(x_ref, yrw_ref, ylru_ref, pg_ref, wrw_ref, wlru_ref, wmix_ref, gb_ref, g_ref, b_ref, o_ref):
    o_rw = _dot(yrw_ref[...], wrw_ref[...])
    o_lru = _dot(ylru_ref[...], wlru_ref[...])
    gates = _sigmoid(pg_ref[...] + gb_ref[...])
    mix = _dot(gates[:, 0:D_MODEL] * o_rw + gates[:, D_MODEL:2 * D_MODEL] * o_lru, wmix_ref[...])
    o_ref[...] = _layer_norm(ALPHA * x_ref[...] + mix, g_ref[...], b_ref[...])


def _layer_spec(a, l):
    return pl.BlockSpec((None,) + a.shape[1:], lambda *_: (l, 0, 0))


def _post(x, y_rw, y_lru, p_gate, l, w_rw, w_lru, w_mix, gate_b, g, b, tm):
    m = x.shape[0]
    rows = lambda width: pl.BlockSpec((tm, width), lambda i: (i, 0))
    const = lambda a: _layer_spec(a, l)
    return pl.pallas_call(
        _post_kernel,
        grid=(m // tm,),
        in_specs=[rows(D_MODEL), rows(D_MODEL), rows(D_MODEL), rows(2 * D_MODEL),
                  const(w_rw), const(w_lru), const(w_mix), const(gate_b), const(g), const(b)],
        out_specs=rows(D_MODEL),
        out_shape=jax.ShapeDtypeStruct((m, D_MODEL), F32),
        compiler_params=_params(("parallel",)),
        name="post",
    )(x, y_rw, y_lru, p_gate, w_rw, w_lru, w_mix, gate_b, g, b)


def _attn_kernel(q_ref, k_ref, v_ref, o_ref, *, tq):
    q = q_ref[0]
    if tq < 8:
        q = jnp.broadcast_to(q[0:1, :], (8, D_MODEL))
    for h in range(XA_HEADS):
        sl = slice(h * XA_HEAD, (h + 1) * XA_HEAD)
        s = _dot(q[:, sl], k_ref[0, :, sl], _NT) * (XA_HEAD ** -0.5)
        e = jnp.exp(s - jnp.max(s, -1, keepdims=True))
        prob = e / jnp.sum(e, -1, keepdims=True)
        o = _dot(prob, v_ref[0, :, sl])
        o_ref[0, :, sl] = o[0:tq, :]


def _attn(q, k, v, tq):
    b, t, _ = q.shape
    return pl.pallas_call(
        functools.partial(_attn_kernel, tq=tq),
        grid=(b, t // tq),
        in_specs=[pl.BlockSpec((1, tq, D_MODEL), lambda i, j: (i, j, 0)),
                  pl.BlockSpec((1, N_MEM, D_MODEL), lambda i, j: (i, 0, 0)),
                  pl.BlockSpec((1, N_MEM, D_MODEL), lambda i, j: (i, 0, 0))],
        out_specs=pl.BlockSpec((1, tq, D_MODEL), lambda i, j: (i, j, 0)),
        out_shape=jax.ShapeDtypeStruct(q.shape, F32),
        compiler_params=_params(("parallel", "parallel")),
        name="attn",
    )(q, k, v)


def _attn_decode_kernel(q_ref, k_ref, v_ref, o_ref):
    q = q_ref[...]
    s = jnp.sum(k_ref[...] * q[:, None, :, :], axis=-1, keepdims=True) * (XA_HEAD ** -0.5)
    e = jnp.exp(s - jnp.max(s, axis=1, keepdims=True))
    prob = e / jnp.sum(e, axis=1, keepdims=True)
    o_ref[...] = jnp.sum(prob * v_ref[...], axis=1)


def _attn_decode(q, k, v, l, nb):
    n = q.shape[0]
    kv = pl.BlockSpec((None, nb, N_MEM, XA_HEADS, XA_HEAD), lambda i: (l, i, 0, 0, 0))
    qo = pl.BlockSpec((nb, XA_HEADS, XA_HEAD), lambda i: (i, 0, 0))
    return pl.pallas_call(
        _attn_decode_kernel,
        grid=(n // nb,),
        in_specs=[qo, kv, kv],
        out_specs=qo,
        out_shape=jax.ShapeDtypeStruct(q.shape, F32),
        compiler_params=_params(("parallel",)),
        name="attn_decode",
    )(q, k, v)


def _mm_ln_kernel(a_ref, w_ref, x_ref, g_ref, b_ref, o_ref):
    o_ref[...] = _layer_norm(ALPHA * x_ref[...] + _dot(a_ref[...], w_ref[...]), g_ref[...], b_ref[...])


def _mm_ln(a, l, w, x, g, b, tm):
    m = a.shape[0]
    rows = pl.BlockSpec((tm, D_MODEL), lambda i: (i, 0))
    const = lambda z: _layer_spec(z, l)
    return pl.pallas_call(
        _mm_ln_kernel,
        grid=(m // tm,),
        in_specs=[rows, const(w), rows, const(g), const(b)],
        out_specs=rows,
        out_shape=jax.ShapeDtypeStruct((m, D_MODEL), F32),
        compiler_params=_params(("parallel",)),
        name="mm_ln",
    )(a, w, x, g, b)


def _mlp_kernel(x_ref, up_ref, down_ref, g_ref, b_ref, o_ref, acc_ref):
    f = pl.program_id(1)
    x = x_ref[...]
    hdn = jnp.square(jnp.maximum(_dot(x, up_ref[...]), 0.0))
    part = _dot(hdn, down_ref[...])

    @pl.when(f == 0)
    def _():
        acc_ref[...] = part

    @pl.when(f > 0)
    def _():
        acc_ref[...] += part

    @pl.when(f == pl.num_programs(1) - 1)
    def _():
        o_ref[...] = _layer_norm(ALPHA * x + acc_ref[...], g_ref[...], b_ref[...])


def _mlp(x, l, up, down, g, b, tm, tf):
    m = x.shape[0]
    return pl.pallas_call(
        _mlp_kernel,
        grid=(m // tm, D_FF // tf),
        in_specs=[pl.BlockSpec((tm, D_MODEL), lambda i, f: (i, 0)),
                  pl.BlockSpec((None, D_MODEL, tf), lambda i, f: (l, 0, f)),
                  pl.BlockSpec((None, tf, D_MODEL), lambda i, f: (l, f, 0)),
                  _layer_spec(g, l), _layer_spec(b, l)],
        out_specs=pl.BlockSpec((tm, D_MODEL), lambda i, f: (i, 0)),
        out_shape=jax.ShapeDtypeStruct((m, D_MODEL), F32),
        scratch_shapes=[pltpu.VMEM((tm, D_MODEL), F32)],
        compiler_params=_params(("parallel", "arbitrary")),
        name="mlp",
    )(x, up, down, g, b)


def _row(x):
    return x.reshape(1, -1)


def _pair_blocks(wb):
    z = jnp.zeros((N_PAIR, RW_HEAD, RW_HEAD), wb.dtype)
    even, odd = wb[0::2], wb[1::2]
    top = jnp.concatenate([even, z], axis=2)
    bot = jnp.concatenate([z, odd], axis=2)
    return jnp.concatenate([top, bot], axis=1).astype(BF16)


def _layer_weights(l, W):
    z64 = jnp.zeros((64, D_MODEL), F32)
    rw = [_row(W['mu_shift'][l]), _row(W['rw_w0'][l]),
          jnp.concatenate([W['rw_w2'][l], z64], 0).astype(BF16), _row(W['rw_a0'][l]),
          jnp.concatenate([z64, W['rw_a2'][l]], 0).astype(BF16), W['rw_g2'][l].astype(BF16),
          _row(W['rw_k_k'][l]), _row(W['rw_k_a'][l]), _row(W['rw_r_k'][l]),
          _row(W['rw_lnx_g'][l]), _row(W['rw_lnx_b'][l])]
    lru = [W['lru_conv_w'][l], _row(W['lru_conv_b'][l]), _pair_blocks(W['lru_wa'][l]), _row(W['lru_ba'][l]),
           _pair_blocks(W['lru_wx'][l]), _row(W['lru_bx'][l]), _row(W['lru_lambda'][l])]
    return rw + lru


def _dense_weights(W):
    vec = lambda a: a.reshape(DEPTH, 1, -1)
    w_in = W['w_in']
    return dict(
        w_rw=w_in[:, :, 0:N_SHIFT].astype(BF16), w_uy=w_in[:, :, N_SHIFT:N_SHIFT + C_UY].astype(BF16),
        w_gate=w_in[:, :, N_SHIFT + C_UY:].astype(BF16),
        rw_proj=W['rw_proj'].astype(BF16), lru_proj=W['lru_proj'].astype(BF16), w_mix=W['w_out_mix'].astype(BF16),
        gate_b=vec(W['mix_gate_b']), wq=W['xa_wq'].astype(BF16), wo=W['xa_wo'].astype(BF16),
        up=W['mlp_up'].astype(BF16), down=W['mlp_down'].astype(BF16),
        ln1_g=vec(W['ln1_g']), ln1_b=vec(W['ln1_b']), ln2_g=vec(W['ln2_g']), ln2_b=vec(W['ln2_b']),
        ln3_g=vec(W['ln3_g']), ln3_b=vec(W['ln3_b']))


def _trunk(x, mem_k, mem_v, states, W, P, tm, tq):
    batch, seq, _ = x.shape
    m = batch * seq
    x2 = x.reshape(m, D_MODEL)
    o_rw_st, o_shift_st, o_h_st, o_conv_st = [], [], [], []
    for l in range(DEPTH):
        mixw = _layer_weights(l, W)
        p_rw = _mm(x2, P['w_rw'], l, tm, N_SHIFT, "proj_rw")
        p_uy = _mm(x2, P['w_uy'], l, tm, C_UY, "proj_uy")
        p_gate = _mm(x2, P['w_gate'], l, tm, 2 * D_MODEL, "proj_gate")
        if states is None:
            y_rw, y_lru, s_new, h_new = _mixer_prompt(p_rw, p_uy, mixw, batch, seq)
            h_new = h_new.reshape(batch, D_MODEL)
            conv_new = p_uy.reshape(batch, seq, C_UY)[:, seq - (CONV_W - 1):, 0:D_MODEL]
        else:
            s_rw, s_shift, s_h, s_conv = states
            rw7, y_lru, h_new = _mixer_sample_prep(
                p_rw, s_shift[l], p_uy, s_conv[l].reshape(batch, (CONV_W - 1) * D_MODEL), s_h[l], mixw)
            hd = (RW_HEADS, 1, RW_HEAD)
            s_new, y_rw = _rwkv_step(s_rw, l, rw7.reshape((7, batch) + hd), W['rw_r_k'][l].reshape(hd),
                                     W['rw_lnx_g'][l].reshape(hd), W['rw_lnx_b'][l].reshape(hd), 8)
            y_rw = y_rw.reshape(batch, D_MODEL)
            conv_new = jnp.concatenate([s_conv[l][:, 1:], p_uy[:, None, 0:D_MODEL]], axis=1)
        x2 = _post(x2, y_rw, y_lru, p_gate, l, P['rw_proj'], P['lru_proj'], P['w_mix'], P['gate_b'],
                   P['ln1_g'], P['ln1_b'], min(tm, 256))
        q = _mm(x2, P['wq'], l, tm, 1024, "proj_q")
        if states is None:
            o = _attn(q.reshape(batch, seq, D_MODEL), mem_k[l], mem_v[l], tq)
        else:
            o = _attn_decode(q.reshape(batch, XA_HEADS, XA_HEAD), mem_k, mem_v, l, 4)
        x2 = _mm_ln(o.reshape(m, D_MODEL), l, P['wo'], x2, P['ln2_g'], P['ln2_b'], min(tm, 512))
        x2 = _mlp(x2, l, P['up'], P['down'], P['ln3_g'], P['ln3_b'], tm, 1024)
        o_rw_st.append(s_new)
        o_shift_st.append(p_rw.reshape(batch, seq, N_SHIFT)[:, seq - 1])
        o_h_st.append(h_new)
        o_conv_st.append(conv_new)
    return x2.reshape(batch, seq, D_MODEL), (jnp.stack(o_rw_st, 0), jnp.stack(o_shift_st, 0),
                                             jnp.stack(o_h_st, 0), jnp.stack(o_conv_st, 0))


def kernel(x_prompt, x_sample, mem_prompt, cache_mem_k, cache_mem_v, state_rwkv, state_rwkv_shift, state_lru_h, state_lru_conv, w_in, mu_shift, rw_w0, rw_w2, rw_a0, rw_a2, rw_g2, rw_k_k, rw_k_a, rw_r_k, rw_lnx_g, rw_lnx_b, rw_proj, lru_conv_w, lru_conv_b, lru_wa, lru_ba, lru_wx, lru_bx, lru_lambda, lru_proj, mix_gate_b, w_out_mix, ln1_g, ln1_b, xa_wq, xa_wk, xa_wv, xa_wo, ln2_g, ln2_b, mlp_up, mlp_down, ln3_g, ln3_b):
    W = dict(w_in=w_in, mu_shift=mu_shift, rw_w0=rw_w0, rw_w2=rw_w2, rw_a0=rw_a0, rw_a2=rw_a2,
             rw_g2=rw_g2, rw_k_k=rw_k_k, rw_k_a=rw_k_a, rw_r_k=rw_r_k, rw_lnx_g=rw_lnx_g,
             rw_lnx_b=rw_lnx_b, rw_proj=rw_proj, lru_conv_w=lru_conv_w, lru_conv_b=lru_conv_b,
             lru_wa=lru_wa, lru_ba=lru_ba, lru_wx=lru_wx, lru_bx=lru_bx, lru_lambda=lru_lambda,
             lru_proj=lru_proj, mix_gate_b=mix_gate_b, w_out_mix=w_out_mix, ln1_g=ln1_g, ln1_b=ln1_b,
             xa_wq=xa_wq, xa_wo=xa_wo, ln2_g=ln2_g, ln2_b=ln2_b, mlp_up=mlp_up, mlp_down=mlp_down,
             ln3_g=ln3_g, ln3_b=ln3_b)
    bp, seq, _ = x_prompt.shape
    bs = x_sample.shape[0]
    mem2 = mem_prompt.reshape(bp * N_MEM, D_MODEL)
    tmem = min(512, bp * N_MEM)
    wk, wv = xa_wk.astype(BF16), xa_wv.astype(BF16)
    mem_k_p = jnp.stack([_mm(mem2, wk, l, tmem, 1024, "mem_k") for l in range(DEPTH)], 0)
    mem_v_p = jnp.stack([_mm(mem2, wv, l, tmem, 1024, "mem_v") for l in range(DEPTH)], 0)
    mem_k_p = mem_k_p.reshape(DEPTH, bp, N_MEM, D_MODEL)
    mem_v_p = mem_v_p.reshape(DEPTH, bp, N_MEM, D_MODEL)
    P = _dense_weights(W)
    tm_p = min(512, bp * seq)
    y_prompt, (p_rw, p_shift, p_h, p_conv) = _trunk(x_prompt, mem_k_p, mem_v_p, None, W, P, tm_p, min(512, seq))
    y_sample, (s_rw, s_shift, s_h, s_conv) = _trunk(
        x_sample, cache_mem_k, cache_mem_v, (state_rwkv, state_rwkv_shift, state_lru_h, state_lru_conv),
        W, P, bs, 1)
    kv_shape = (DEPTH, bp, N_MEM, XA_HEADS, XA_HEAD)
    return (y_prompt, y_sample, p_rw, p_shift, p_h, p_conv, mem_k_p.reshape(kv_shape), mem_v_p.reshape(kv_shape),
            s_rw, s_shift, s_h, s_conv)
```

```python
import functools

import jax
import jax.numpy as jnp
from jax import lax
from jax.experimental import pallas as pl
from jax.experimental.pallas import tpu as pltpu

F32 = jnp.float32
BF16 = jnp.bfloat16

D_MODEL = 1024
DEPTH = 2
RW_HEAD = 64
RW_HEADS = 16
N_PAIR = 8
LANES = 128
R_GATE = 128
GN_EPS = 64e-5
LRU_C = 8.0
CONV_W = 4
N_MEM = 256
XA_HEADS = 4
XA_HEAD = 256
D_FF = 4096
ALPHA = (2 * DEPTH) ** 0.25
LN_EPS = 1e-5
N_SHIFT = 3328
C_UY = 2048
CHUNK = 64
SUB = 16
SCAN_PARTS = 1
OUT_PARTS = 1
VMEM_LIMIT = 56 * 1024 * 1024

_NT = (((1,), (1,)), ((), ()))
_TN = (((0,), (0,)), ((), ()))


def _params(sem):
    return pltpu.CompilerParams(dimension_semantics=sem, vmem_limit_bytes=VMEM_LIMIT)


def _sigmoid(x):
    return 1.0 / (1.0 + jnp.exp(-x))


def _softplus(x):
    return jnp.maximum(x, 0.0) + jnp.log(1.0 + jnp.exp(-jnp.abs(x)))


def _gelu_tanh(x):
    return 0.5 * x * (1.0 + jnp.tanh(0.7978845608028654 * (x + 0.044715 * (x * x * x))))


def _layer_norm(z, g, b):
    mu = jnp.mean(z, -1, keepdims=True)
    zc = z - mu
    var = jnp.mean(zc * zc, -1, keepdims=True)
    return zc * lax.rsqrt(var + LN_EPS) * g + b


def _dot(a, b, dims=None):
    a = a.astype(BF16)
    b = b.astype(BF16)
    if dims is None:
        return jnp.dot(a, b, preferred_element_type=F32)
    return lax.dot_general(a, b, dims, preferred_element_type=F32)


def _split(a):
    hi = a.astype(BF16)
    lo = (a - hi.astype(F32)).astype(BF16)
    return hi, lo


def _parts(a, n):
    return _split(a) if n == 2 else (a.astype(BF16),)


def _cat(parts_list, axis):
    return tuple(jnp.concatenate(xs, axis=axis) for xs in zip(*parts_list))


def _mmp(a, b, dims=None):
    acc = _dot(a[0], b[0], dims)
    if len(b) == 2:
        acc = acc + _dot(a[0], b[1], dims)
    if len(a) == 2:
        acc = acc + _dot(a[1], b[0], dims)
    return acc


def _dot2x(a, b_exact):
    ah, al = _split(a)
    return _dot(ah, b_exact) + _dot(al, b_exact)


def _pair_ones():
    r = lax.broadcasted_iota(jnp.int32, (LANES, LANES), 0) // RW_HEAD
    c = lax.broadcasted_iota(jnp.int32, (LANES, LANES), 1) // RW_HEAD
    return jnp.where(r == c, 1.0, 0.0).astype(BF16)


def _segsum(x, ones):
    cols = [_dot2x(x[:, p * LANES:(p + 1) * LANES], ones) for p in range(x.shape[1] // LANES)]
    return jnp.concatenate(cols, axis=1) if len(cols) > 1 else cols[0]


def _blockdot(x, w_ref):
    cols = [_dot(x[:, p * LANES:(p + 1) * LANES], w_ref[p]) for p in range(N_PAIR)]
    return jnp.concatenate(cols, axis=1)


def _rwkv_elem(ps, w, ones):
    r = ps[:, 0:1024]
    k = ps[:, 1024:2048]
    v = ps[:, 2048:3072]
    slab = ps[:, 3072:3200]
    xg = ps[:, 3200:3328]
    wlog = -_softplus(-(w['w0'][...] + _dot(jnp.tanh(slab), w['w2'][...]))) - 0.5
    ld = -jnp.exp(wlog)
    a_sig = _sigmoid(w['a0'][...] + _dot(slab, w['a2'][...]))
    g = _dot(_sigmoid(xg), w['g2'][...])
    kk = k * w['k_k'][...]
    kk = kk * lax.rsqrt(jnp.maximum(_segsum(kk * kk, ones), 1e-24))
    k_mod = k * (1.0 + (a_sig - 1.0) * w['k_a'][...])
    return r, k_mod, v, ld, -kk, kk * a_sig, g


def _lru_elem(xc, y_in, w):
    gr = _sigmoid(_blockdot(xc, w['wa']) + w['ba'][...])
    gi = _sigmoid(_blockdot(xc, w['wx']) + w['bx'][...])
    log_a = -LRU_C * _softplus(-w['lam'][...]) * gr
    a = jnp.exp(log_a)
    bterm = jnp.sqrt(1.0 - jnp.exp(2.0 * log_a)) * (gi * xc)
    return a, bterm, _gelu_tanh(y_in)


_RW_NAMES = ('mu', 'w0', 'w2', 'a0', 'a2', 'g2', 'k_k', 'k_a', 'r_k', 'ln_g', 'ln_b')
_LRU_NAMES = ('cw', 'cb', 'wa', 'ba', 'wx', 'bx', 'lam')


def _stack(x, m0):
    return jnp.concatenate([jnp.where(m0, x, 0.0), jnp.where(m0, 0.0, x)], axis=0)


def _mixer_prompt_kernel(*refs):
    n_w = len(_RW_NAMES) + len(_LRU_NAMES)
    prw_ref, puy_ref = refs[0], refs[1]
    w = dict(zip(_RW_NAMES + _LRU_NAMES, refs[2:2 + n_w]))
    yrw_ref, ylru_ref, sout_ref, hout_ref = refs[2 + n_w:6 + n_w]
    pbuf, ubuf, s_scr, h_scr = refs[6 + n_w:]
    C = CHUNK
    c = pl.program_id(1)

    @pl.when(c == 0)
    def _():
        pbuf[0:8, :] = jnp.zeros((8, N_SHIFT), F32)
        ubuf[0:8, :] = jnp.zeros((8, D_MODEL), F32)
        s_scr[...] = jnp.zeros_like(s_scr)
        h_scr[...] = jnp.zeros_like(h_scr)

    ones = _pair_ones()
    row = lax.broadcasted_iota(jnp.int32, (C, 1), 0)

    p = prw_ref[...]
    pbuf[8:8 + C, :] = p
    prev = pbuf[7:7 + C, :]
    ps = p + w['mu'][...] * (prev - p)
    r, k_mod, v, ld, a_vec, b_vec, g = _rwkv_elem(ps, w, ones)
    pbuf[0:8, :] = p[C - 8:C, :]

    tri = jnp.where(lax.broadcasted_iota(jnp.int32, (C, C), 0) >= lax.broadcasted_iota(jnp.int32, (C, C), 1),
                    1.0, 0.0).astype(BF16)
    l_hi = ld.astype(BF16)
    l_r1 = ld - l_hi.astype(F32)
    l_mid = l_r1.astype(BF16)
    l_lo = (l_r1 - l_mid.astype(F32)).astype(BF16)
    cum = _dot(tri, l_hi) + (_dot(tri, l_mid) + _dot(tri, l_lo))
    cum_last = cum[C - 1:C, :]
    e_neg = jnp.exp(-cum)
    at = a_vec * jnp.exp(cum - ld)
    rt = r * jnp.exp(cum)
    bt = b_vec * e_neg
    kt = k_mod * e_neg
    e_tail = jnp.exp(cum_last - cum)
    bh = b_vec * e_tail
    kh = k_mod * e_tail
    wc = jnp.exp(cum_last)
    bonus_dot = _segsum(r * k_mod * w['r_k'][...], ones)

    n = 2 * C
    ri = lax.broadcasted_iota(jnp.int32, (n, n), 0)
    ci = lax.broadcasted_iota(jnp.int32, (n, n), 1)
    low_strict = ri > ci
    low_incl = ri >= ci
    diag_blk = (ri // SUB) == (ci // SUB)
    eye = jnp.where(ri == ci, 1.0, 0.0)
    m0 = lax.broadcasted_iota(jnp.int32, (C, LANES), 1) < RW_HEAD

    pairs = range(N_PAIR)
    sls = [slice(hp * LANES, (hp + 1) * LANES) for hp in pairs]
    pa, pr = SCAN_PARTS, OUT_PARTS

    def wide(xs, ys_list, px=pa, py=pa):
        outs = [[None] * N_PAIR for _ in ys_list]
        for p in pairs:
            o = _mmp(_parts(xs[p], px), _cat([_parts(ys[p], py) for ys in ys_list], 1))
            for i in range(len(ys_list)):
                outs[i][p] = o[:, i * n:(i + 1) * n]
        return outs

    vs_ = [_stack(v[:, sl], m0) for sl in sls]
    s_bd = [s_scr[hp] for hp in pairs]
    rhs = [_cat([_parts(_stack(bt[:, sl], m0), pa), _parts(_stack(kt[:, sl], m0), pa), _parts(s_bd[hp], pa)], 0)
           for hp, sl in zip(pairs, sls)]
    if pa == pr:
        gram = [_mmp(_cat([_parts(at[:, sl], pa), _parts(rt[:, sl], pr)], 0), rhs[hp], _NT)
                for hp, sl in zip(pairs, sls)]
    else:
        gram = [jnp.concatenate([_mmp(_parts(at[:, sl], pa), rhs[hp], _NT),
                                 _mmp(_parts(rt[:, sl], pr), rhs[hp][:pr], _NT)], axis=0)
                for hp, sl in zip(pairs, sls)]
    n_ab = [jnp.where(low_strict, _stack(gm[0:C, 0:n], m0), 0.0) for gm in gram]
    a_ak = [jnp.where(low_strict, _stack(gm[0:C, n:2 * n], m0), 0.0) for gm in gram]
    a_s = [_stack(gm[0:C, 2 * n:3 * n], m0) for gm in gram]
    r_bk = [jnp.concatenate([jnp.where(low_incl, _stack(gm[C:n, 0:n], m0), 0.0),
                             jnp.where(low_incl, _stack(gm[C:n, n:2 * n], m0), 0.0)], axis=1) for gm in gram]
    r_s = [_stack(gm[C:n, 2 * n:3 * n], m0) for gm in gram]
    rhs_u = [x + y for x, y in zip(a_s, wide(a_ak, [vs_])[0])]

    n_d = [jnp.where(diag_blk, x, 0.0) for x in n_ab]
    n_o = [x - d for x, d in zip(n_ab, n_d)]
    d_inv = [eye + d for d in n_d]
    q = wide(n_d, [n_d])[0]
    for _ in range(2):
        q, qd = wide(q, [q, d_inv])
        d_inv = [d + x for d, x in zip(d_inv, qd)]
    d_inv = [d + x for d, x in zip(d_inv, wide(q, [d_inv])[0])]
    m1, z = wide(d_inv, [n_o, rhs_u])
    m2, mz = wide(m1, [m1, z])
    z = [x + y for x, y in zip(z, mz)]
    u = [x + y for x, y in zip(z, wide(m2, [z])[0])]
    uv =[jnp.concatenate([x, vv], axis=0) for x, vv in zip(u, vs_)]
    y_s = [rs + _mmp(_parts(rb, pr), _parts(x, pr)) for rs, rb, x in zip(r_s, r_bk, uv)]
    s_new = [s_bd[hp] * wc[:, sl]
             + _mmp(_parts(uv[hp], pa),
                    _parts(jnp.concatenate([_stack(bh[:, sl], m0), _stack(kh[:, sl], m0)], axis=0), pa), _TN)
             for hp, sl in zip(pairs, sls)]
    for hp in pairs:
        s_scr[hp] = s_new[hp]

    y = [x[0:C] + x[C:n] for x in y_s]
    mu = [_dot2x(x, ones) * (1.0 / RW_HEAD) for x in y]
    yc = [x - m for x, m in zip(y, mu)]
    var = [_dot2x(x * x, ones) * (1.0 / RW_HEAD) for x in yc]
    for hp, sl in zip(pairs, sls):
        yn = yc[hp] * lax.rsqrt(var[hp] + GN_EPS) * w['ln_g'][:, sl] + w['ln_b'][:, sl]
        yrw_ref[:, sl] = (yn + bonus_dot[:, sl] * v[:, sl]) * g[:, sl]

    u_in = puy_ref[:, 0:D_MODEL]
    ubuf[8:8 + C, :] = u_in
    cw = w['cw']
    xc = (w['cb'][...] + ubuf[5:5 + C, :] * cw[0:1, :] + ubuf[6:6 + C, :] * cw[1:2, :]
          + ubuf[7:7 + C, :] * cw[2:3, :] + u_in * cw[3:4, :])
    ubuf[0:8, :] = u_in[C - 8:C, :]
    a_l, b_l, gel = _lru_elem(xc, puy_ref[:, D_MODEL:2 * D_MODEL], w)
    d = 1
    while d < C:
        keep = row >= d
        a_sh = jnp.where(keep, pltpu.roll(a_l, d, 0), 1.0)
        b_sh = jnp.where(keep, pltpu.roll(b_l, d, 0), 0.0)
        b_l = a_l * b_sh + b_l
        a_l = a_l * a_sh
        d *= 2
    h = a_l * h_scr[0:1, :] + b_l
    h_scr[0:1, :] = h[C - 1:C, :]
    ylru_ref[...] = h * gel

    @pl.when(c == pl.num_programs(1) - 1)
    def _():
        hout_ref[0] = h[C - 1:C, :]
        for hp in range(N_PAIR):
            s_bd = s_scr[hp]
            sout_ref[0, 2 * hp] = s_bd[0:RW_HEAD, 0:RW_HEAD]
            sout_ref[0, 2 * hp + 1] = s_bd[RW_HEAD:LANES, RW_HEAD:LANES]


def _mixer_prompt(p_rw, p_uy, weights, batch, seq):
    C = CHUNK
    nc = seq // C
    row_map = lambda b, c: (b * nc + c, 0)
    w_specs = [pl.BlockSpec(x.shape, (lambda b, c, nd=x.ndim: (0,) * nd)) for x in weights]
    return pl.pallas_call(
        _mixer_prompt_kernel,
        grid=(batch, nc),
        in_specs=[pl.BlockSpec((C, N_SHIFT), row_map), pl.BlockSpec((C, C_UY), row_map)] + w_specs,
        out_specs=[pl.BlockSpec((C, D_MODEL), row_map), pl.BlockSpec((C, D_MODEL), row_map),
                   pl.BlockSpec((1, RW_HEADS, RW_HEAD, RW_HEAD), lambda b, c: (b, 0, 0, 0)),
                   pl.BlockSpec((1, 1, D_MODEL), lambda b, c: (b, 0, 0))],
        out_shape=[jax.ShapeDtypeStruct((batch * seq, D_MODEL), F32),
                   jax.ShapeDtypeStruct((batch * seq, D_MODEL), F32),
                   jax.ShapeDtypeStruct((batch, RW_HEADS, RW_HEAD, RW_HEAD), F32),
                   jax.ShapeDtypeStruct((batch, 1, D_MODEL), F32)],
        scratch_shapes=[pltpu.VMEM((8 + C, N_SHIFT), F32), pltpu.VMEM((8 + C, D_MODEL), F32),
                        pltpu.VMEM((N_PAIR, LANES, LANES), F32), pltpu.VMEM((8, D_MODEL), F32)],
        compiler_params=_params(("parallel", "arbitrary")),
        name="mixer_prompt",
    )(p_rw, p_uy, *weights)


def _mixer_sample_prep_kernel(*refs):
    n_w = len(_RW_NAMES) + len(_LRU_NAMES)
    prw_ref, shift_ref, puy_ref, conv_ref, h0_ref = refs[0:5]
    w = dict(zip(_RW_NAMES + _LRU_NAMES, refs[5:5 + n_w]))
    rw_ref, ylru_ref, hnew_ref = refs[5 + n_w:]
    ones = _pair_ones()
    p = prw_ref[...]
    ps = p + w['mu'][...] * (shift_ref[...] - p)
    r, k_mod, v, ld, a_vec, b_vec, g = _rwkv_elem(ps, w, ones)
    rw_ref[0] = r
    rw_ref[1] = k_mod
    rw_ref[2] = v
    rw_ref[3] = jnp.exp(ld)
    rw_ref[4] = a_vec
    rw_ref[5] = b_vec
    rw_ref[6] = g
    u_in = puy_ref[:, 0:D_MODEL]
    cw = w['cw']
    xc = (w['cb'][...] + conv_ref[:, 0:D_MODEL] * cw[0:1, :] + conv_ref[:, D_MODEL:2 * D_MODEL] * cw[1:2, :]
          + conv_ref[:, 2 * D_MODEL:3 * D_MODEL] * cw[2:3, :] + u_in * cw[3:4, :])
    a_l, b_l, gel = _lru_elem(xc, puy_ref[:, D_MODEL:2 * D_MODEL], w)
    h = a_l * h0_ref[...] + b_l
    hnew_ref[...] = h
    ylru_ref[...] = h * gel


def _mixer_sample_prep(p_rw, shift, p_uy, conv, h0, weights):
    nb = p_rw.shape[0]
    ins = [p_rw, shift, p_uy, conv, h0] + list(weights)
    full = lambda x: pl.BlockSpec(x.shape, (lambda i, nd=x.ndim: (0,) * nd))
    return pl.pallas_call(
        _mixer_sample_prep_kernel,
        grid=(1,),
        in_specs=[full(x) for x in ins],
        out_specs=[pl.BlockSpec((7, nb, D_MODEL), lambda i: (0, 0, 0)),
                   pl.BlockSpec((nb, D_MODEL), lambda i: (0, 0)),
                   pl.BlockSpec((nb, D_MODEL), lambda i: (0, 0))],
        out_shape=[jax.ShapeDtypeStruct((7, nb, D_MODEL), F32),
                   jax.ShapeDtypeStruct((nb, D_MODEL), F32),
                   jax.ShapeDtypeStruct((nb, D_MODEL), F32)],
        compiler_params=_params(("arbitrary",)),
        name="mixer_sample_prep",
    )(*ins)


def _rwkv_step_kernel(s_ref, x_ref, rk_ref, lng_ref, lnb_ref, sout_ref, y_ref):
    s = s_ref[...]
    r, k, v, wd, a, b, g = (x_ref[i] for i in range(7))
    ri = lax.broadcasted_iota(jnp.int32, (RW_HEAD, RW_HEAD), 0)
    ci = lax.broadcasted_iota(jnp.int32, (RW_HEAD, RW_HEAD), 1)
    eye = jnp.where(ri == ci, 1.0, 0.0)
    sa = jnp.sum(s * a, axis=-1, keepdims=True)
    v_col = jnp.sum(eye * v, axis=-1, keepdims=True)
    s_new = s * wd + sa * b + v_col * k
    sout_ref[...] = s_new
    y_col = jnp.sum(s_new * r, axis=-1, keepdims=True)
    y = jnp.sum(eye * y_col, axis=-2, keepdims=True)
    mu = jnp.mean(y, -1, keepdims=True)
    yc = y - mu
    var = jnp.mean(yc * yc, -1, keepdims=True)
    yn = yc * lax.rsqrt(var + GN_EPS) * lng_ref[...] + lnb_ref[...]
    bonus = jnp.sum(r * k * rk_ref[...], -1, keepdims=True) * v
    y_ref[...] = (yn + bonus) * g


def _rwkv_step(state, l, rw7, r_k, ln_g, ln_b, nb):
    n = state.shape[1]
    hd = (RW_HEADS, 1, RW_HEAD)
    return pl.pallas_call(
        _rwkv_step_kernel,
        grid=(n // nb,),
        in_specs=[pl.BlockSpec((None, nb, RW_HEADS, RW_HEAD, RW_HEAD), lambda i: (l, i, 0, 0, 0)),
                  pl.BlockSpec((7, nb) + hd, lambda i: (0, i, 0, 0, 0)),
                  pl.BlockSpec(hd, lambda i: (0, 0, 0)), pl.BlockSpec(hd, lambda i: (0, 0, 0)),
                  pl.BlockSpec(hd, lambda i: (0, 0, 0))],
        out_specs=[pl.BlockSpec((nb, RW_HEADS, RW_HEAD, RW_HEAD), lambda i: (i, 0, 0, 0)),
                   pl.BlockSpec((nb,) + hd, lambda i: (i, 0, 0, 0))],
        out_shape=[jax.ShapeDtypeStruct(state.shape[1:], F32), jax.ShapeDtypeStruct((n,) + hd, F32)],
        compiler_params=_params(("parallel",)),
        name="rwkv_step",
    )(state, rw7, r_k, ln_g, ln_b)


def _mm_kernel(x_ref, w_ref, o_ref):
    o_ref[...] = _dot(x_ref[...], w_ref[...])


def _mm(x, w, l, tm, tn, name):
    m, kd = x.shape
    n = w.shape[2]
    return pl.pallas_call(
        _mm_kernel,
        grid=(m // tm, n // tn),
        in_specs=[pl.BlockSpec((tm, kd), lambda i, j: (i, 0)),
                  pl.BlockSpec((None, kd, tn), lambda i, j: (l, 0, j))],
        out_specs=pl.BlockSpec((tm, tn), lambda i, j: (i, j)),
        out_shape=jax.ShapeDtypeStruct((m, n), F32),
        compiler_params=_params(("parallel", "parallel")),
        name=name,
    )(x, w)


def _post_kernel(x_ref, yrw_ref, ylru_ref, pg_ref, wrw_ref, wlru_ref, wmix_ref, gb_ref, g_ref, b_ref, wq_ref,
                 o_ref, q_ref):
    o_rw = _dot(yrw_ref[...], wrw_ref[...])
    o_lru = _dot(ylru_ref[...], wlru_ref[...])
    gates = _sigmoid(pg_ref[...] + gb_ref[...])
    mix = _dot(gates[:, 0:D_MODEL] * o_rw + gates[:, D_MODEL:2 * D_MODEL] * o_lru, wmix_ref[...])
    x1 = _layer_norm(ALPHA * x_ref[...] + mix, g_ref[...], b_ref[...])
    o_ref[...] = x1
    q_ref[...] = _dot(x1, wq_ref[...])


def _layer_spec(a, l):
    return pl.BlockSpec((None,) + a.shape[1:], lambda *_: (l, 0, 0))


def _post(x, y_rw, y_lru, p_gate, l, w_rw, w_lru, w_mix, gate_b, g, b, wq, tm):
    m = x.shape[0]
    rows = lambda width: pl.BlockSpec((tm, width), lambda i: (i, 0))
    const = lambda a: _layer_spec(a, l)
    return pl.pallas_call(
        _post_kernel,
        grid=(m // tm,),
        in_specs=[rows(D_MODEL), rows(D_MODEL), rows(D_MODEL), rows(2 * D_MODEL),
                  const(w_rw), const(w_lru), const(w_mix), const(gate_b), const(g), const(b), const(wq)],
        out_specs=[rows(D_MODEL), rows(D_MODEL)],
        out_shape=[jax.ShapeDtypeStruct((m, D_MODEL), F32), jax.ShapeDtypeStruct((m, D_MODEL), F32)],
        compiler_params=_params(("parallel",)),
        name="post",
    )(x, y_rw, y_lru, p_gate, w_rw, w_lru, w_mix, gate_b, g, b, wq)


def _attn_kernel(q_ref, k_ref, v_ref, o_ref, *, tq):
    q = q_ref[0]
    if tq < 8:
        q = jnp.broadcast_to(q[0:1, :], (8, D_MODEL))
    for h in range(XA_HEADS):
        sl = slice(h * XA_HEAD, (h + 1) * XA_HEAD)
        s = _dot(q[:, sl], k_ref[0, :, sl], _NT) * (XA_HEAD ** -0.5)
        e = jnp.exp(s - jnp.max(s, -1, keepdims=True))
        prob = e / jnp.sum(e, -1, keepdims=True)
        o = _dot(prob, v_ref[0, :, sl])
        o_ref[0, :, sl] = o[0:tq, :]


def _attn(q, k, v, tq):
    b, t, _ = q.shape
    return pl.pallas_call(
        functools.partial(_attn_kernel, tq=tq),
        grid=(b, t // tq),
        in_specs=[pl.BlockSpec((1, tq, D_MODEL), lambda i, j: (i, j, 0)),
                  pl.BlockSpec((1, N_MEM, D_MODEL), lambda i, j: (i, 0, 0)),
                  pl.BlockSpec((1, N_MEM, D_MODEL), lambda i, j: (i, 0, 0))],
        out_specs=pl.BlockSpec((1, tq, D_MODEL), lambda i, j: (i, j, 0)),
        out_shape=jax.ShapeDtypeStruct(q.shape, F32),
        compiler_params=_params(("parallel", "parallel")),
        name="attn",
    )(q, k, v)


def _attn_decode_kernel(q_ref, k_ref, v_ref, o_ref):
    q = q_ref[...]
    s = jnp.sum(k_ref[...] * q[:, None, :, :], axis=-1, keepdims=True) * (XA_HEAD ** -0.5)
    e = jnp.exp(s - jnp.max(s, axis=1, keepdims=True))
    prob = e / jnp.sum(e, axis=1, keepdims=True)
    o_ref[...] = jnp.sum(prob * v_ref[...], axis=1)


def _attn_decode(q, k, v, l, nb):
    n = q.shape[0]
    kv = pl.BlockSpec((None, nb, N_MEM, XA_HEADS, XA_HEAD), lambda i: (l, i, 0, 0, 0))
    qo = pl.BlockSpec((nb, XA_HEADS, XA_HEAD), lambda i: (i, 0, 0))
    return pl.pallas_call(
        _attn_decode_kernel,
        grid=(n // nb,),
        in_specs=[qo, kv, kv],
        out_specs=qo,
        out_shape=jax.ShapeDtypeStruct(q.shape, F32),
        compiler_params=_params(("parallel",)),
        name="attn_decode",
    )(q, k, v)


def _tail_kernel(a_ref, x_ref, wo_ref, g2_ref, b2_ref, up_ref, down_ref, g3_ref, b3_ref, o_ref):
    x2 = _layer_norm(ALPHA * x_ref[...] + _dot(a_ref[...], wo_ref[...]), g2_ref[...], b2_ref[...])
    hdn = jnp.square(jnp.maximum(_dot(x2, up_ref[...]), 0.0))
    o_ref[...] = _layer_norm(ALPHA * x2 + _dot(hdn, down_ref[...]), g3_ref[...], b3_ref[...])


def _tail(a, x, l, wo, g2, b2, up, down, g3, b3, tm):
    m = a.shape[0]
    rows = pl.BlockSpec((tm, D_MODEL), lambda i: (i, 0))
    const = lambda z: _layer_spec(z, l)
    return pl.pallas_call(
        _tail_kernel,
        grid=(m // tm,),
        in_specs=[rows, rows, const(wo), const(g2), const(b2), const(up), const(down), const(g3), const(b3)],
        out_specs=rows,
        out_shape=jax.ShapeDtypeStruct((m, D_MODEL), F32),
        compiler_params=_params(("parallel",)),
        name="tail",
    )(a, x, wo, g2, b2, up, down, g3, b3)


def _row(x):
    return x.reshape(1, -1)


def _pair_blocks(wb):
    z = jnp.zeros((N_PAIR, RW_HEAD, RW_HEAD), wb.dtype)
    even, odd = wb[0::2], wb[1::2]
    top = jnp.concatenate([even, z], axis=2)
    bot = jnp.concatenate([z, odd], axis=2)
    return jnp.concatenate([top, bot], axis=1).astype(BF16)


def _layer_weights(l, W):
    z64 = jnp.zeros((64, D_MODEL), F32)
    rw = [_row(W['mu_shift'][l]), _row(W['rw_w0'][l]),
          jnp.concatenate([W['rw_w2'][l], z64], 0).astype(BF16), _row(W['rw_a0'][l]),
          jnp.concatenate([z64, W['rw_a2'][l]], 0).astype(BF16), W['rw_g2'][l].astype(BF16),
          _row(W['rw_k_k'][l]), _row(W['rw_k_a'][l]), _row(W['rw_r_k'][l]),
          _row(W['rw_lnx_g'][l]), _row(W['rw_lnx_b'][l])]
    lru = [W['lru_conv_w'][l], _row(W['lru_conv_b'][l]), _pair_blocks(W['lru_wa'][l]), _row(W['lru_ba'][l]),
           _pair_blocks(W['lru_wx'][l]), _row(W['lru_bx'][l]), _row(W['lru_lambda'][l])]
    return rw + lru


def _dense_weights(W):
    vec = lambda a: a.reshape(DEPTH, 1, -1)
    w_in = W['w_in']
    return dict(
        w_rw=w_in[:, :, 0:N_SHIFT].astype(BF16), w_uy=w_in[:, :, N_SHIFT:N_SHIFT + C_UY].astype(BF16),
        w_gate=w_in[:, :, N_SHIFT + C_UY:].astype(BF16),
        rw_proj=W['rw_proj'].astype(BF16), lru_proj=W['lru_proj'].astype(BF16), w_mix=W['w_out_mix'].astype(BF16),
        gate_b=vec(W['mix_gate_b']), wq=W['xa_wq'].astype(BF16), wo=W['xa_wo'].astype(BF16),
        up=W['mlp_up'].astype(BF16), down=W['mlp_down'].astype(BF16),
        ln1_g=vec(W['ln1_g']), ln1_b=vec(W['ln1_b']), ln2_g=vec(W['ln2_g']), ln2_b=vec(W['ln2_b']),
        ln3_g=vec(W['ln3_g']), ln3_b=vec(W['ln3_b']))


def _trunk(x, mem_k, mem_v, states, W, P, tm, tq):
    batch, seq, _ = x.shape
    m = batch * seq
    x2 = x.reshape(m, D_MODEL)
    o_rw_st, o_shift_st, o_h_st, o_conv_st = [], [], [], []
    for l in range(DEPTH):
        mixw = _layer_weights(l, W)
        p_rw = _mm(x2, P['w_rw'], l, tm, N_SHIFT, "proj_rw")
        p_uy = _mm(x2, P['w_uy'], l, tm, C_UY, "proj_uy")
        p_gate = _mm(x2, P['w_gate'], l, tm, 2 * D_MODEL, "proj_gate")
        if states is None:
            y_rw, y_lru, s_new, h_new = _mixer_prompt(p_rw, p_uy, mixw, batch, seq)
            h_new = h_new.reshape(batch, D_MODEL)
            conv_new = p_uy.reshape(batch, seq, C_UY)[:, seq - (CONV_W - 1):, 0:D_MODEL]
        else:
            s_rw, s_shift, s_h, s_conv = states
            rw7, y_lru, h_new = _mixer_sample_prep(
                p_rw, s_shift[l], p_uy, s_conv[l].reshape(batch, (CONV_W - 1) * D_MODEL), s_h[l], mixw)
            hd = (RW_HEADS, 1, RW_HEAD)
            s_new, y_rw = _rwkv_step(s_rw, l, rw7.reshape((7, batch) + hd), W['rw_r_k'][l].reshape(hd),
                                     W['rw_lnx_g'][l].reshape(hd), W['rw_lnx_b'][l].reshape(hd), 8)
            y_rw = y_rw.reshape(batch, D_MODEL)
            conv_new = jnp.concatenate([s_conv[l][:, 1:], p_uy[:, None, 0:D_MODEL]], axis=1)
        x2, q = _post(x2, y_rw, y_lru, p_gate, l, P['rw_proj'], P['lru_proj'], P['w_mix'], P['gate_b'],
                      P['ln1_g'], P['ln1_b'], P['wq'], min(tm, 256))
        if states is None:
            o = _attn(q.reshape(batch, seq, D_MODEL), mem_k[l], mem_v[l], tq)
        else:
            o = _attn_decode(q.reshape(batch, XA_HEADS, XA_HEAD), mem_k, mem_v, l, 4)
        x2 = _tail(o.reshape(m, D_MODEL), x2, l, P['wo'], P['ln2_g'], P['ln2_b'], P['up'], P['down'],
                   P['ln3_g'], P['ln3_b'], tm)
        o_rw_st.append(s_new)
        o_shift_st.append(p_rw.reshape(batch, seq, N_SHIFT)[:, seq - 1])
        o_h_st.append(h_new)
        o_conv_st.append(conv_new)
    return x2.reshape(batch, seq, D_MODEL), (jnp.stack(o_rw_st, 0), jnp.stack(o_shift_st, 0),
                                             jnp.stack(o_h_st, 0), jnp.stack(o_conv_st, 0))


def kernel(x_prompt, x_sample, mem_prompt, cache_mem_k, cache_mem_v, state_rwkv, state_rwkv_shift, state_lru_h, state_lru_conv, w_in, mu_shift, rw_w0, rw_w2, rw_a0, rw_a2, rw_g2, rw_k_k, rw_k_a, rw_r_k, rw_lnx_g, rw_lnx_b, rw_proj, lru_conv_w, lru_conv_b, lru_wa, lru_ba, lru_wx, lru_bx, lru_lambda, lru_proj, mix_gate_b, w_out_mix, ln1_g, ln1_b, xa_wq, xa_wk, xa_wv, xa_wo, ln2_g, ln2_b, mlp_up, mlp_down, ln3_g, ln3_b):
    W = dict(w_in=w_in, mu_shift=mu_shift, rw_w0=rw_w0, rw_w2=rw_w2, rw_a0=rw_a0, rw_a2=rw_a2,
             rw_g2=rw_g2, rw_k_k=rw_k_k, rw_k_a=rw_k_a, rw_r_k=rw_r_k, rw_lnx_g=rw_lnx_g,
             rw_lnx_b=rw_lnx_b, rw_proj=rw_proj, lru_conv_w=lru_conv_w, lru_conv_b=lru_conv_b,
             lru_wa=lru_wa, lru_ba=lru_ba, lru_wx=lru_wx, lru_bx=lru_bx, lru_lambda=lru_lambda,
             lru_proj=lru_proj, mix_gate_b=mix_gate_b, w_out_mix=w_out_mix, ln1_g=ln1_g, ln1_b=ln1_b,
             xa_wq=xa_wq, xa_wo=xa_wo, ln2_g=ln2_g, ln2_b=ln2_b, mlp_up=mlp_up, mlp_down=mlp_down,
             ln3_g=ln3_g, ln3_b=ln3_b)
    bp, seq, _ = x_prompt.shape
    bs = x_sample.shape[0]
    mem2 = mem_prompt.reshape(bp * N_MEM, D_MODEL)
    tmem = min(512, bp * N_MEM)
    wk, wv = xa_wk.astype(BF16), xa_wv.astype(BF16)
    mem_k_p = jnp.stack([_mm(mem2, wk, l, tmem, 1024, "mem_k") for l in range(DEPTH)], 0)
    mem_v_p = jnp.stack([_mm(mem2, wv, l, tmem, 1024, "mem_v") for l in range(DEPTH)], 0)
    mem_k_p = mem_k_p.reshape(DEPTH, bp, N_MEM, D_MODEL)
    mem_v_p = mem_v_p.reshape(DEPTH, bp, N_MEM, D_MODEL)
    P = _dense_weights(W)
    tm_p = min(512, bp * seq)
    y_prompt, (p_rw, p_shift, p_h, p_conv) = _trunk(x_prompt, mem_k_p, mem_v_p, None, W, P, tm_p, min(512, seq))
    y_sample, (s_rw, s_shift, s_h, s_conv) = _trunk(
        x_sample, cache_mem_k, cache_mem_v, (state_rwkv, state_rwkv_shift, state_lru_h, state_lru_conv),
        W, P, bs, 1)
    kv_shape = (DEPTH, bp, N_MEM, XA_HEADS, XA_HEAD)
    return (y_prompt, y_sample, p_rw, p_shift, p_h, p_conv, mem_k_p.reshape(kv_shape), mem_v_p.reshape(kv_shape),
            s_rw, s_shift, s_h, s_conv)
```

```python
import functools

import jax
import jax.numpy as jnp
from jax import lax
from jax.experimental import pallas as pl
from jax.experimental.pallas import tpu as pltpu

F32 = jnp.float32
BF16 = jnp.bfloat16

D_MODEL = 1024
DEPTH = 2
RW_HEAD = 64
RW_HEADS = 16
N_PAIR = 8
LANES = 128
R_GATE = 128
GN_EPS = 64e-5
LRU_C = 8.0
CONV_W = 4
N_MEM = 256
XA_HEADS = 4
XA_HEAD = 256
D_FF = 4096
ALPHA = (2 * DEPTH) ** 0.25
LN_EPS = 1e-5
N_SHIFT = 3328
C_UY = 2048
CHUNK = 64
SUB = 16
SCAN_PARTS = 1
OUT_PARTS = 1
VMEM_LIMIT = 56 * 1024 * 1024

_NT = (((1,), (1,)), ((), ()))
_TN = (((0,), (0,)), ((), ()))


def _params(sem):
    return pltpu.CompilerParams(dimension_semantics=sem, vmem_limit_bytes=VMEM_LIMIT)


def _sigmoid(x):
    return 1.0 / (1.0 + jnp.exp(-x))


def _softplus(x):
    return jnp.maximum(x, 0.0) + jnp.log(1.0 + jnp.exp(-jnp.abs(x)))


def _gelu_tanh(x):
    return 0.5 * x * (1.0 + jnp.tanh(0.7978845608028654 * (x + 0.044715 * (x * x * x))))


def _layer_norm(z, g, b):
    mu = jnp.mean(z, -1, keepdims=True)
    zc = z - mu
    var = jnp.mean(zc * zc, -1, keepdims=True)
    return zc * lax.rsqrt(var + LN_EPS) * g + b


def _dot(a, b, dims=None):
    a = a.astype(BF16)
    b = b.astype(BF16)
    if dims is None:
        return jnp.dot(a, b, preferred_element_type=F32)
    return lax.dot_general(a, b, dims, preferred_element_type=F32)


def _split(a):
    hi = a.astype(BF16)
    lo = (a - hi.astype(F32)).astype(BF16)
    return hi, lo


def _parts(a, n):
    return _split(a) if n == 2 else (a.astype(BF16),)


def _cat(parts_list, axis):
    return tuple(jnp.concatenate(xs, axis=axis) for xs in zip(*parts_list))


def _mmp(a, b, dims=None):
    acc = _dot(a[0], b[0], dims)
    if len(b) == 2:
        acc = acc + _dot(a[0], b[1], dims)
    if len(a) == 2:
        acc = acc + _dot(a[1], b[0], dims)
    return acc


def _dot2x(a, b_exact):
    ah, al = _split(a)
    return _dot(ah, b_exact) + _dot(al, b_exact)


def _pair_ones():
    r = lax.broadcasted_iota(jnp.int32, (LANES, LANES), 0) // RW_HEAD
    c = lax.broadcasted_iota(jnp.int32, (LANES, LANES), 1) // RW_HEAD
    return jnp.where(r == c, 1.0, 0.0).astype(BF16)


def _segsum(x, ones):
    cols = [_dot2x(x[:, p * LANES:(p + 1) * LANES], ones) for p in range(x.shape[1] // LANES)]
    return jnp.concatenate(cols, axis=1) if len(cols) > 1 else cols[0]


def _blockdot(x, w_ref):
    cols = [_dot(x[:, p * LANES:(p + 1) * LANES], w_ref[p]) for p in range(N_PAIR)]
    return jnp.concatenate(cols, axis=1)


def _rwkv_elem(ps, w, ones):
    r = ps[:, 0:1024]
    k = ps[:, 1024:2048]
    v = ps[:, 2048:3072]
    slab = ps[:, 3072:3200]
    xg = ps[:, 3200:3328]
    wlog = -_softplus(-(w['w0'][...] + _dot(jnp.tanh(slab), w['w2'][...]))) - 0.5
    ld = -jnp.exp(wlog)
    a_sig = _sigmoid(w['a0'][...] + _dot(slab, w['a2'][...]))
    g = _dot(_sigmoid(xg), w['g2'][...])
    kk = k * w['k_k'][...]
    kk = kk * lax.rsqrt(jnp.maximum(_segsum(kk * kk, ones), 1e-24))
    k_mod = k * (1.0 + (a_sig - 1.0) * w['k_a'][...])
    return r, k_mod, v, ld, -kk, kk * a_sig, g


def _lru_elem(xc, y_in, w):
    gr = _sigmoid(_blockdot(xc, w['wa']) + w['ba'][...])
    gi = _sigmoid(_blockdot(xc, w['wx']) + w['bx'][...])
    log_a = -LRU_C * _softplus(-w['lam'][...]) * gr
    a = jnp.exp(log_a)
    bterm = jnp.sqrt(1.0 - jnp.exp(2.0 * log_a)) * (gi * xc)
    return a, bterm, _gelu_tanh(y_in)


_RW_NAMES = ('mu', 'w0', 'w2', 'a0', 'a2', 'g2', 'k_k', 'k_a', 'r_k', 'ln_g', 'ln_b')
_LRU_NAMES = ('cw', 'cb', 'wa', 'ba', 'wx', 'bx', 'lam')


def _stack(x, m0):
    return jnp.concatenate([jnp.where(m0, x, 0.0), jnp.where(m0, 0.0, x)], axis=0)


def _mixer_prompt_kernel(*refs):
    n_w = len(_RW_NAMES) + len(_LRU_NAMES)
    prw_ref, puy_ref = refs[0], refs[1]
    w = dict(zip(_RW_NAMES + _LRU_NAMES, refs[2:2 + n_w]))
    yrw_ref, ylru_ref, sout_ref, hout_ref = refs[2 + n_w:6 + n_w]
    pbuf, ubuf, s_scr, h_scr = refs[6 + n_w:]
    C = CHUNK
    c = pl.program_id(1)

    @pl.when(c == 0)
    def _():
        pbuf[0:8, :] = jnp.zeros((8, N_SHIFT), F32)
        ubuf[0:8, :] = jnp.zeros((8, D_MODEL), F32)
        s_scr[...] = jnp.zeros_like(s_scr)
        h_scr[...] = jnp.zeros_like(h_scr)

    ones = _pair_ones()
    row = lax.broadcasted_iota(jnp.int32, (C, 1), 0)

    p = prw_ref[...]
    pbuf[8:8 + C, :] = p
    prev = pbuf[7:7 + C, :]
    ps = p + w['mu'][...] * (prev - p)
    r, k_mod, v, ld, a_vec, b_vec, g = _rwkv_elem(ps, w, ones)
    pbuf[0:8, :] = p[C - 8:C, :]

    tri = jnp.where(lax.broadcasted_iota(jnp.int32, (C, C), 0) >= lax.broadcasted_iota(jnp.int32, (C, C), 1),
                    1.0, 0.0).astype(BF16)
    l_hi = ld.astype(BF16)
    l_r1 = ld - l_hi.astype(F32)
    l_mid = l_r1.astype(BF16)
    l_lo = (l_r1 - l_mid.astype(F32)).astype(BF16)
    cum = _dot(tri, l_hi) + (_dot(tri, l_mid) + _dot(tri, l_lo))
    cum_last = cum[C - 1:C, :]
    e_neg = jnp.exp(-cum)
    at = a_vec * jnp.exp(cum - ld)
    rt = r * jnp.exp(cum)
    bt = b_vec * e_neg
    kt = k_mod * e_neg
    e_tail = jnp.exp(cum_last - cum)
    bh = b_vec * e_tail
    kh = k_mod * e_tail
    wc = jnp.exp(cum_last)
    bonus_dot = _segsum(r * k_mod * w['r_k'][...], ones)

    n = 2 * C
    ri = lax.broadcasted_iota(jnp.int32, (n, n), 0)
    ci = lax.broadcasted_iota(jnp.int32, (n, n), 1)
    low_strict = ri > ci
    low_incl = ri >= ci
    diag_blk = (ri // SUB) == (ci // SUB)
    eye = jnp.where(ri == ci, 1.0, 0.0)
    m0 = lax.broadcasted_iota(jnp.int32, (C, LANES), 1) < RW_HEAD

    pairs = range(N_PAIR)
    sls = [slice(hp * LANES, (hp + 1) * LANES) for hp in pairs]
    pa, pr = SCAN_PARTS, OUT_PARTS

    def wide(xs, ys_list, px=pa, py=pa):
        outs = [[None] * N_PAIR for _ in ys_list]
        for p in pairs:
            o = _mmp(_parts(xs[p], px), _cat([_parts(ys[p], py) for ys in ys_list], 1))
            for i in range(len(ys_list)):
                outs[i][p] = o[:, i * n:(i + 1) * n]
        return outs

    vs_ = [_stack(v[:, sl], m0) for sl in sls]
    s_bd = [s_scr[hp] for hp in pairs]
    rhs = [_cat([_parts(_stack(bt[:, sl], m0), pa), _parts(_stack(kt[:, sl], m0), pa), _parts(s_bd[hp], pa)], 0)
           for hp, sl in zip(pairs, sls)]
    if pa == pr:
        gram = [_mmp(_cat([_parts(at[:, sl], pa), _parts(rt[:, sl], pr)], 0), rhs[hp], _NT)
                for hp, sl in zip(pairs, sls)]
    else:
        gram = [jnp.concatenate([_mmp(_parts(at[:, sl], pa), rhs[hp], _NT),
                                 _mmp(_parts(rt[:, sl], pr), rhs[hp][:pr], _NT)], axis=0)
                for hp, sl in zip(pairs, sls)]
    n_ab = [jnp.where(low_strict, _stack(gm[0:C, 0:n], m0), 0.0) for gm in gram]
    a_ak = [jnp.where(low_strict, _stack(gm[0:C, n:2 * n], m0), 0.0) for gm in gram]
    a_s = [_stack(gm[0:C, 2 * n:3 * n], m0) for gm in gram]
    r_bk = [jnp.concatenate([jnp.where(low_incl, _stack(gm[C:n, 0:n], m0), 0.0),
                             jnp.where(low_incl, _stack(gm[C:n, n:2 * n], m0), 0.0)], axis=1) for gm in gram]
    r_s = [_stack(gm[C:n, 2 * n:3 * n], m0) for gm in gram]
    rhs_u = [x + y for x, y in zip(a_s, wide(a_ak, [vs_])[0])]

    n_d = [jnp.where(diag_blk, x, 0.0) for x in n_ab]
    n_o = [x - d for x, d in zip(n_ab, n_d)]
    d_inv = [eye + d for d in n_d]
    q = wide(n_d, [n_d])[0]
    for _ in range(2):
        q, qd = wide(q, [q, d_inv])
        d_inv = [d + x for d, x in zip(d_inv, qd)]
    d_inv = [d + x for d, x in zip(d_inv, wide(q, [d_inv])[0])]
    m1, z = wide(d_inv, [n_o, rhs_u])
    m2, mz = wide(m1, [m1, z])
    z = [x + y for x, y in zip(z, mz)]
    u = [x + y for x, y in zip(z, wide(m2, [z])[0])]
    uv = [jnp.concatenate([x, vv], axis=0) for x, vv in zip(u, vs_)]
    y_s = [rs + _mmp(_parts(rb, pr), _parts(x, pr)) for rs, rb, x in zip(r_s, r_bk, uv)]
    s_new = [s_bd[hp] * wc[:, sl]
             + _mmp(_parts(uv[hp], pa),
                    _parts(jnp.concatenate([_stack(bh[:, sl], m0), _stack(kh[:, sl], m0)], axis=0), pa), _TN)
             for hp, sl in zip(pairs, sls)]
    for hp in pairs:
        s_scr[hp] = s_new[hp]

    y = [x[0:C] + x[C:n] for x in y_s]
    mu = [_dot2x(x, ones) * (1.0 / RW_HEAD) for x in y]
    yc = [x - m for x, m in zip(y, mu)]
    var = [_dot2x(x * x, ones) * (1.0 / RW_HEAD) for x in yc]
    for hp, sl in zip(pairs, sls):
        yn = yc[hp] * lax.rsqrt(var[hp] + GN_EPS) * w['ln_g'][:, sl] + w['ln_b'][:, sl]
        yrw_ref[:, sl] = (yn + bonus_dot[:, sl] * v[:, sl]) * g[:, sl]

    u_in = puy_ref[:, 0:D_MODEL]
    ubuf[8:8 + C, :] = u_in
    cw = w['cw']
    xc = (w['cb'][...] + ubuf[5:5 + C, :] * cw[0:1, :] + ubuf[6:6 + C, :] * cw[1:2, :]
          + ubuf[7:7 + C, :] * cw[2:3, :] + u_in * cw[3:4, :])
    ubuf[0:8, :] = u_in[C - 8:C, :]
    a_l, b_l, gel = _lru_elem(xc, puy_ref[:, D_MODEL:2 * D_MODEL], w)
    d = 1
    while d < C:
        keep = row >= d
        a_sh = jnp.where(keep, pltpu.roll(a_l, d, 0), 1.0)
        b_sh = jnp.where(keep, pltpu.roll(b_l, d, 0), 0.0)
        b_l = a_l * b_sh + b_l
        a_l = a_l * a_sh
        d *= 2
    h = a_l * h_scr[0:1, :] + b_l
    h_scr[0:1, :] = h[C - 1:C, :]
    ylru_ref[...] = h * gel

    @pl.when(c == pl.num_programs(1) - 1)
    def _():
        hout_ref[0] = h[C - 1:C, :]
        for hp in range(N_PAIR):
            s_bd = s_scr[hp]
            sout_ref[0, 2 * hp] = s_bd[0:RW_HEAD, 0:RW_HEAD]
            sout_ref[0, 2 * hp + 1] = s_bd[RW_HEAD:LANES, RW_HEAD:LANES]


def _mixer_prompt(p_rw, p_uy, weights, batch, seq):
    C = CHUNK
    nc = seq // C
    row_map = lambda b, c: (b * nc + c, 0)
    w_specs = [pl.BlockSpec(x.shape, (lambda b, c, nd=x.ndim: (0,) * nd)) for x in weights]
    return pl.pallas_call(
        _mixer_prompt_kernel,
        grid=(batch, nc),
        in_specs=[pl.BlockSpec((C, N_SHIFT), row_map), pl.BlockSpec((C, C_UY), row_map)] + w_specs,
        out_specs=[pl.BlockSpec((C, D_MODEL), row_map), pl.BlockSpec((C, D_MODEL), row_map),
                   pl.BlockSpec((1, RW_HEADS, RW_HEAD, RW_HEAD), lambda b, c: (b, 0, 0, 0)),
                   pl.BlockSpec((1, 1, D_MODEL), lambda b, c: (b, 0, 0))],
        out_shape=[jax.ShapeDtypeStruct((batch * seq, D_MODEL), F32),
                   jax.ShapeDtypeStruct((batch * seq, D_MODEL), F32),
                   jax.ShapeDtypeStruct((batch, RW_HEADS, RW_HEAD, RW_HEAD), F32),
                   jax.ShapeDtypeStruct((batch, 1, D_MODEL), F32)],
        scratch_shapes=[pltpu.VMEM((8 + C, N_SHIFT), F32), pltpu.VMEM((8 + C, D_MODEL), F32),
                        pltpu.VMEM((N_PAIR, LANES, LANES), F32), pltpu.VMEM((8, D_MODEL), F32)],
        compiler_params=_params(("parallel", "arbitrary")),
        name="mixer_prompt",
    )(p_rw, p_uy, *weights)


def _mixer_sample_prep_kernel(*refs):
    n_w = len(_RW_NAMES) + len(_LRU_NAMES)
    prw_ref, shift_ref, puy_ref, conv_ref, h0_ref = refs[0:5]
    w = dict(zip(_RW_NAMES + _LRU_NAMES, refs[5:5 + n_w]))
    rw_ref, ylru_ref, hnew_ref = refs[5 + n_w:]
    ones = _pair_ones()
    p = prw_ref[...]
    ps = p + w['mu'][...] * (shift_ref[...] - p)
    r, k_mod, v, ld, a_vec, b_vec, g = _rwkv_elem(ps, w, ones)
    for i, x in enumerate((r, k_mod, v, jnp.exp(ld), a_vec, b_vec, g)):
        rw_ref[i] = x
    u_in = puy_ref[:, 0:D_MODEL]
    cw = w['cw']
    xc = (w['cb'][...] + conv_ref[:, 0:D_MODEL] * cw[0:1, :] + conv_ref[:, D_MODEL:2 * D_MODEL] * cw[1:2, :]
          + conv_ref[:, 2 * D_MODEL:3 * D_MODEL] * cw[2:3, :] + u_in * cw[3:4, :])
    a_l, b_l, gel = _lru_elem(xc, puy_ref[:, D_MODEL:2 * D_MODEL], w)
    h = a_l * h0_ref[...] + b_l
    hnew_ref[...] = h
    ylru_ref[...] = h * gel


def _mixer_sample_prep(p_rw, shift, p_uy, conv, h0, weights):
    nb = p_rw.shape[0]
    ins = [p_rw, shift, p_uy, conv, h0] + list(weights)
    full = lambda x: pl.BlockSpec(x.shape, (lambda i, nd=x.ndim: (0,) * nd))
    return pl.pallas_call(
        _mixer_sample_prep_kernel,
        grid=(1,),
        in_specs=[full(x) for x in ins],
        out_specs=[pl.BlockSpec((7, nb, D_MODEL), lambda i: (0, 0, 0)),
                   pl.BlockSpec((nb, D_MODEL), lambda i: (0, 0)),
                   pl.BlockSpec((nb, D_MODEL), lambda i: (0, 0))],
        out_shape=[jax.ShapeDtypeStruct((7, nb, D_MODEL), F32),
                   jax.ShapeDtypeStruct((nb, D_MODEL), F32),
                   jax.ShapeDtypeStruct((nb, D_MODEL), F32)],
        compiler_params=_params(("arbitrary",)),
        name="mixer_sample_prep",
    )(*ins)


def _rwkv_step_kernel(*refs, n_carry):
    if n_carry:
        s_ref, x_ref, rk_ref, lng_ref, lnb_ref, carry_ref, sout_ref, y_ref = refs
        sout_ref[0:n_carry] = carry_ref[...]
    else:
        s_ref, x_ref, rk_ref, lng_ref, lnb_ref, sout_ref, y_ref = refs
    s = s_ref[0]
    r, k, v, wd, a, b, g = (x_ref[i, 0] for i in range(7))
    sa = jnp.sum(s * a[None, :, :], axis=1)
    s_new = s * wd[None, :, :] + sa[:, None, :] * b[None, :, :] + v[:, None, :] * k[None, :, :]
    sout_ref[n_carry, 0] = s_new
    y = jnp.sum(s_new * r[None, :, :], axis=1)
    mu = jnp.mean(y, axis=0, keepdims=True)
    yc = y - mu
    var = jnp.mean(yc * yc, axis=0, keepdims=True)
    yn = yc * lax.rsqrt(var + GN_EPS) * lng_ref[0] + lnb_ref[0]
    bonus = jnp.sum(r * k * rk_ref[0], axis=0, keepdims=True) * v
    y_ref[0] = (yn + bonus) * g


def _rwkv_step(state, l, rw7, r_k, ln_g, ln_b, carry):
    n = state.shape[-1]
    blk = (RW_HEAD, RW_HEAD, n)
    head = lambda *lead: pl.BlockSpec(lead + (RW_HEAD, 1), lambda i: (i, 0, 0))
    in_specs = [pl.BlockSpec((None, 1) + blk, lambda i: (l, i, 0, 0, 0)),
                pl.BlockSpec((7, 1, RW_HEAD, n), lambda i: (0, i, 0, 0)),
                head(1), head(1), head(1)]
    args = [state, rw7, r_k, ln_g, ln_b]
    if carry is not None:
        in_specs.append(pl.BlockSpec((l, 1) + blk, lambda i: (0, i, 0, 0, 0)))
        args.append(carry)
    return pl.pallas_call(
        functools.partial(_rwkv_step_kernel, n_carry=0 if carry is None else l),
        grid=(RW_HEADS,),
        in_specs=in_specs,
        out_specs=[pl.BlockSpec((l + 1, 1) + blk, lambda i: (0, i, 0, 0, 0)),
                   pl.BlockSpec((1, RW_HEAD, n), lambda i: (i, 0, 0))],
        out_shape=[jax.ShapeDtypeStruct((l + 1, RW_HEADS) + blk, F32),
                   jax.ShapeDtypeStruct((RW_HEADS, RW_HEAD, n), F32)],
        compiler_params=_params(("parallel",)),
        name="rwkv_step",
    )(*args)


def _mm_kernel(x_ref, w_ref, o_ref):
    o_ref[...] = _dot(x_ref[...], w_ref[...])


def _mm(x, w, l, tm, tn, name):
    m, kd = x.shape
    n = w.shape[2]
    return pl.pallas_call(
        _mm_kernel,
        grid=(m // tm, n // tn),
        in_specs=[pl.BlockSpec((tm, kd), lambda i, j: (i, 0)),
                  pl.BlockSpec((None, kd, tn), lambda i, j: (l, 0, j))],
        out_specs=pl.BlockSpec((tm, tn), lambda i, j: (i, j)),
        out_shape=jax.ShapeDtypeStruct((m, n), F32),
        compiler_params=_params(("parallel", "parallel")),
        name=name,
    )(x, w)


def _post_kernel(x_ref, yrw_ref, ylru_ref, pg_ref, wrw_ref, wlru_ref, wmix_ref, gb_ref, g_ref, b_ref, wq_ref,
                 o_ref, q_ref):
    o_rw = _dot(yrw_ref[...], wrw_ref[...])
    o_lru = _dot(ylru_ref[...], wlru_ref[...])
    gates = _sigmoid(pg_ref[...] + gb_ref[...])
    mix = _dot(gates[:, 0:D_MODEL] * o_rw + gates[:, D_MODEL:2 * D_MODEL] * o_lru, wmix_ref[...])
    x1 = _layer_norm(ALPHA * x_ref[...] + mix, g_ref[...], b_ref[...])
    o_ref[...] = x1
    q_ref[...] = _dot(x1, wq_ref[...])


def _layer_spec(a, l):
    return pl.BlockSpec((None,) + a.shape[1:], lambda *_: (l, 0, 0))


def _post(x, y_rw, y_lru, p_gate, l, w_rw, w_lru, w_mix, gate_b, g, b, wq, tm):
    m = x.shape[0]
    rows = lambda width: pl.BlockSpec((tm, width), lambda i: (i, 0))
    const = lambda a: _layer_spec(a, l)
    return pl.pallas_call(
        _post_kernel,
        grid=(m // tm,),
        in_specs=[rows(D_MODEL), rows(D_MODEL), rows(D_MODEL), rows(2 * D_MODEL),
                  const(w_rw), const(w_lru), const(w_mix), const(gate_b), const(g), const(b), const(wq)],
        out_specs=[rows(D_MODEL), rows(D_MODEL)],
        out_shape=[jax.ShapeDtypeStruct((m, D_MODEL), F32), jax.ShapeDtypeStruct((m, D_MODEL), F32)],
        compiler_params=_params(("parallel",)),
        name="post",
    )(x, y_rw, y_lru, p_gate, w_rw, w_lru, w_mix, gate_b, g, b, wq)


def _attn_kernel(q_ref, k_ref, v_ref, o_ref):
    heads = [slice(h * XA_HEAD, (h + 1) * XA_HEAD) for h in range(XA_HEADS)]
    s = [_dot(q_ref[0, :, sl], k_ref[0, :, sl], _NT) * (XA_HEAD ** -0.5) for sl in heads]
    e = [jnp.exp(x - jnp.max(x, -1, keepdims=True)) for x in s]
    prob = [x / jnp.sum(x, -1, keepdims=True) for x in e]
    for sl, x in zip(heads, prob):
        o_ref[0, :, sl] = _dot(x, v_ref[0, :, sl])


def _attn(q, k, v, tq):
    b, t, _ = q.shape
    return pl.pallas_call(
        _attn_kernel,
        grid=(b, t // tq),
        in_specs=[pl.BlockSpec((1, tq, D_MODEL), lambda i, j: (i, j, 0)),
                  pl.BlockSpec((1, N_MEM, D_MODEL), lambda i, j: (i, 0, 0)),
                  pl.BlockSpec((1, N_MEM, D_MODEL), lambda i, j: (i, 0, 0))],
        out_specs=pl.BlockSpec((1, tq, D_MODEL), lambda i, j: (i, j, 0)),
        out_shape=jax.ShapeDtypeStruct(q.shape, F32),
        compiler_params=_params(("parallel", "parallel")),
        name="attn",
    )(q, k, v)


def _attn_decode_kernel(q_ref, k_ref, v_ref, o_ref):
    q = q_ref[...] * (XA_HEAD ** -0.5)
    s = jnp.sum(k_ref[...] * q[:, None, :, :], axis=-1, keepdims=True)
    e = jnp.exp(s - jnp.max(s, axis=1, keepdims=True))
    o_ref[...] = jnp.sum(e * v_ref[...], axis=1) / jnp.sum(e, axis=1)


def _attn_decode(q, k, v, l, nb):
    n = q.shape[0]
    kv = pl.BlockSpec((None, nb, N_MEM, XA_HEADS, XA_HEAD), lambda i: (l, i, 0, 0, 0))
    qo = pl.BlockSpec((nb, XA_HEADS, XA_HEAD), lambda i: (i, 0, 0))
    return pl.pallas_call(
        _attn_decode_kernel,
        grid=(n // nb,),
        in_specs=[qo, kv, kv],
        out_specs=qo,
        out_shape=jax.ShapeDtypeStruct(q.shape, F32),
        compiler_params=_params(("parallel",)),
        name="attn_decode",
    )(q, k, v)


def _tail_kernel(a_ref, x_ref, wo_ref, g2_ref, b2_ref, up_ref, down_ref, g3_ref, b3_ref, o_ref):
    x2 = _layer_norm(ALPHA * x_ref[...] + _dot(a_ref[...], wo_ref[...]), g2_ref[...], b2_ref[...])
    hdn = jnp.square(jnp.maximum(_dot(x2, up_ref[...]), 0.0))
    o_ref[...] = _layer_norm(ALPHA * x2 + _dot(hdn, down_ref[...]), g3_ref[...], b3_ref[...])


def _tail(a, x, l, wo, g2, b2, up, down, g3, b3, tm):
    m = a.shape[0]
    rows = pl.BlockSpec((tm, D_MODEL), lambda i: (i, 0))
    const = lambda z: _layer_spec(z, l)
    return pl.pallas_call(
        _tail_kernel,
        grid=(m // tm,),
        in_specs=[rows, rows, const(wo), const(g2), const(b2), const(up), const(down), const(g3), const(b3)],
        out_specs=rows,
        out_shape=jax.ShapeDtypeStruct((m, D_MODEL), F32),
        compiler_params=_params(("parallel",)),
        name="tail",
    )(a, x, wo, g2, b2, up, down, g3, b3)


def _row(x):
    return x.reshape(1, -1)


def _pair_blocks(wb):
    z = jnp.zeros((N_PAIR, RW_HEAD, RW_HEAD), wb.dtype)
    even, odd = wb[0::2], wb[1::2]
    top = jnp.concatenate([even, z], axis=2)
    bot = jnp.concatenate([z, odd], axis=2)
    return jnp.concatenate([top, bot], axis=1).astype(BF16)


def _layer_weights(l, W):
    z64 = jnp.zeros((64, D_MODEL), F32)
    rw = [_row(W['mu_shift'][l]), _row(W['rw_w0'][l]),
          jnp.concatenate([W['rw_w2'][l], z64], 0).astype(BF16), _row(W['rw_a0'][l]),
          jnp.concatenate([z64, W['rw_a2'][l]], 0).astype(BF16), W['rw_g2'][l].astype(BF16),
          _row(W['rw_k_k'][l]), _row(W['rw_k_a'][l]), _row(W['rw_r_k'][l]),
          _row(W['rw_lnx_g'][l]), _row(W['rw_lnx_b'][l])]
    lru = [W['lru_conv_w'][l], _row(W['lru_conv_b'][l]), _pair_blocks(W['lru_wa'][l]), _row(W['lru_ba'][l]),
           _pair_blocks(W['lru_wx'][l]), _row(W['lru_bx'][l]), _row(W['lru_lambda'][l])]
    return rw + lru


def _dense_weights(W):
    vec = lambda a: a.reshape(DEPTH, 1, -1)
    w_in = W['w_in']
    return dict(
        w_rw=w_in[:, :, 0:N_SHIFT].astype(BF16), w_uy=w_in[:, :, N_SHIFT:N_SHIFT + C_UY].astype(BF16),
        w_gate=w_in[:, :, N_SHIFT + C_UY:].astype(BF16),
        rw_proj=W['rw_proj'].astype(BF16), lru_proj=W['lru_proj'].astype(BF16), w_mix=W['w_out_mix'].astype(BF16),
        gate_b=vec(W['mix_gate_b']), wq=W['xa_wq'].astype(BF16), wo=W['xa_wo'].astype(BF16),
        up=W['mlp_up'].astype(BF16), down=W['mlp_down'].astype(BF16),
        ln1_g=vec(W['ln1_g']), ln1_b=vec(W['ln1_b']), ln2_g=vec(W['ln2_g']), ln2_b=vec(W['ln2_b']),
        ln3_g=vec(W['ln3_g']), ln3_b=vec(W['ln3_b']))


def _trunk(x, mem_k, mem_v, states, W, P, tm, tq):
    batch, seq, _ = x.shape
    m = batch * seq
    x2 = x.reshape(m, D_MODEL)
    o_rw_st, o_shift_st, o_h_st, o_conv_st = [], [], [], []
    s_new_prev = None
    if states is not None:
        states = (jnp.transpose(states[0], (0, 2, 3, 4, 1)),) + tuple(states[1:])
    for l in range(DEPTH):
        mixw = _layer_weights(l, W)
        p_rw = _mm(x2, P['w_rw'], l, tm, N_SHIFT, "proj_rw")
        p_uy = _mm(x2, P['w_uy'], l, tm, C_UY, "proj_uy")
        p_gate = _mm(x2, P['w_gate'], l, tm, 2 * D_MODEL, "proj_gate")
        if states is None:
            y_rw, y_lru, s_new, h_new = _mixer_prompt(p_rw, p_uy, mixw, batch, seq)
            h_new = h_new.reshape(batch, D_MODEL)
            conv_new = p_uy.reshape(batch, seq, C_UY)[:, seq - (CONV_W - 1):, 0:D_MODEL]
        else:
            s_rw, s_shift, s_h, s_conv = states
            rw7, y_lru, h_new = _mixer_sample_prep(
                p_rw, s_shift[l], p_uy, s_conv[l].reshape(batch, (CONV_W - 1) * D_MODEL), s_h[l], mixw)
            hd = (RW_HEADS, RW_HEAD, 1)
            rw7 = jnp.transpose(rw7, (0, 2, 1)).reshape(7, RW_HEADS, RW_HEAD, batch)
            s_new, y_rw = _rwkv_step(s_rw, l, rw7, W['rw_r_k'][l].reshape(hd), W['rw_lnx_g'][l].reshape(hd),
                                     W['rw_lnx_b'][l].reshape(hd), s_new_prev)
            s_new_prev = s_new
            y_rw = y_rw.reshape(D_MODEL, batch).T
            conv_new = jnp.concatenate([s_conv[l][:, 1:], p_uy[:, None, 0:D_MODEL]], axis=1)
        x2, q = _post(x2, y_rw, y_lru, p_gate, l, P['rw_proj'], P['lru_proj'], P['w_mix'], P['gate_b'],
                      P['ln1_g'], P['ln1_b'], P['wq'], min(tm, 256))
        if states is None:
            o = _attn(q.reshape(batch, seq, D_MODEL), mem_k[l], mem_v[l], tq)
        else:
            o = _attn_decode(q.reshape(batch, XA_HEADS, XA_HEAD), mem_k, mem_v, l, 4)
        x2 = _tail(o.reshape(m, D_MODEL), x2, l, P['wo'], P['ln2_g'], P['ln2_b'], P['up'], P['down'],
                   P['ln3_g'], P['ln3_b'], tm)
        o_rw_st.append(s_new)
        o_shift_st.append(p_rw.reshape(batch, seq, N_SHIFT)[:, seq - 1])
        o_h_st.append(h_new)
        o_conv_st.append(conv_new)
    rw_state = jnp.stack(o_rw_st, 0) if states is None else jnp.transpose(s_new_prev, (0, 4, 1, 2, 3))
    return x2.reshape(batch, seq, D_MODEL), (rw_state, jnp.stack(o_shift_st, 0),
                                             jnp.stack(o_h_st, 0), jnp.stack(o_conv_st, 0))


def kernel(x_prompt, x_sample, mem_prompt, cache_mem_k, cache_mem_v, state_rwkv, state_rwkv_shift, state_lru_h, state_lru_conv, w_in, mu_shift, rw_w0, rw_w2, rw_a0, rw_a2, rw_g2, rw_k_k, rw_k_a, rw_r_k, rw_lnx_g, rw_lnx_b, rw_proj, lru_conv_w, lru_conv_b, lru_wa, lru_ba, lru_wx, lru_bx, lru_lambda, lru_proj, mix_gate_b, w_out_mix, ln1_g, ln1_b, xa_wq, xa_wk, xa_wv, xa_wo, ln2_g, ln2_b, mlp_up, mlp_down, ln3_g, ln3_b):
    W = dict(w_in=w_in, mu_shift=mu_shift, rw_w0=rw_w0, rw_w2=rw_w2, rw_a0=rw_a0, rw_a2=rw_a2,
             rw_g2=rw_g2, rw_k_k=rw_k_k, rw_k_a=rw_k_a, rw_r_k=rw_r_k, rw_lnx_g=rw_lnx_g,
             rw_lnx_b=rw_lnx_b, rw_proj=rw_proj, lru_conv_w=lru_conv_w, lru_conv_b=lru_conv_b,
             lru_wa=lru_wa, lru_ba=lru_ba, lru_wx=lru_wx, lru_bx=lru_bx, lru_lambda=lru_lambda,
             lru_proj=lru_proj, mix_gate_b=mix_gate_b, w_out_mix=w_out_mix, ln1_g=ln1_g, ln1_b=ln1_b,
             xa_wq=xa_wq, xa_wo=xa_wo, ln2_g=ln2_g, ln2_b=ln2_b, mlp_up=mlp_up, mlp_down=mlp_down,
             ln3_g=ln3_g, ln3_b=ln3_b)
    bp, seq, _ = x_prompt.shape
    bs = x_sample.shape[0]
    mem2 = mem_prompt.reshape(bp * N_MEM, D_MODEL)
    tmem = min(512, bp * N_MEM)
    wk, wv = xa_wk.astype(BF16), xa_wv.astype(BF16)
    mem_k_p = jnp.stack([_mm(mem2, wk, l, tmem, 1024, "mem_k") for l in range(DEPTH)], 0)
    mem_v_p = jnp.stack([_mm(mem2, wv, l, tmem, 1024, "mem_v") for l in range(DEPTH)], 0)
    mem_k_p = mem_k_p.reshape(DEPTH, bp, N_MEM, D_MODEL)
    mem_v_p = mem_v_p.reshape(DEPTH, bp, N_MEM, D_MODEL)
    P = _dense_weights(W)
    tm_p = min(512, bp * seq)
    y_prompt, (p_rw, p_shift, p_h, p_conv) = _trunk(x_prompt, mem_k_p, mem_v_p, None, W, P, tm_p, min(512, seq))
    y_sample, (s_rw, s_shift, s_h, s_conv) = _trunk(
        x_sample, cache_mem_k, cache_mem_v, (state_rwkv, state_rwkv_shift, state_lru_h, state_lru_conv),
        W, P, bs, 1)
    kv_shape = (DEPTH, bp, N_MEM, XA_HEADS, XA_HEAD)
    return (y_prompt, y_sample, p_rw, p_shift, p_h, p_conv, mem_k_p.reshape(kv_shape), mem_v_p.reshape(kv_shape),
            s_rw, s_shift, s_h, s_conv)
```

```python
import functools

import jax
import jax.numpy as jnp
from jax import lax
from jax.experimental import pallas as pl
from jax.experimental.pallas import tpu as pltpu

F32 = jnp.float32
BF16 = jnp.bfloat16

D_MODEL = 1024
DEPTH = 2
RW_HEAD = 64
RW_HEADS = 16
N_PAIR = 8
LANES = 128
R_GATE = 128
GN_EPS = 64e-5
LRU_C = 8.0
CONV_W = 4
N_MEM = 256
XA_HEADS = 4
XA_HEAD = 256
D_FF = 4096
ALPHA = (2 * DEPTH) ** 0.25
LN_EPS = 1e-5
N_SHIFT = 3328
C_UY = 2048
CHUNK = 64
SUB = 16
SCAN_PARTS = 1
OUT_PARTS = 1
DECAY_SCALE = 0.6065306597126334
VMEM_LIMIT = 56 * 1024 * 1024

_NT = (((1,), (1,)), ((), ()))
_TN = (((0,), (0,)), ((), ()))


def _params(sem):
    return pltpu.CompilerParams(dimension_semantics=sem, vmem_limit_bytes=VMEM_LIMIT)


def _sigmoid(x):
    return 1.0 / (1.0 + jnp.exp(-x))


def _softplus(x):
    return jnp.maximum(x, 0.0) + jnp.log(1.0 + jnp.exp(-jnp.abs(x)))


def _gelu_tanh(x):
    return 0.5 * x * (1.0 + jnp.tanh(0.7978845608028654 * (x + 0.044715 * (x * x * x))))


def _layer_norm(z, g, b):
    mu = jnp.mean(z, -1, keepdims=True)
    zc = z - mu
    var = jnp.mean(zc * zc, -1, keepdims=True)
    return zc * lax.rsqrt(var + LN_EPS) * g + b


def _dot(a, b, dims=None):
    a = a.astype(BF16)
    b = b.astype(BF16)
    if dims is None:
        return jnp.dot(a, b, preferred_element_type=F32)
    return lax.dot_general(a, b, dims, preferred_element_type=F32)


def _split(a):
    hi = a.astype(BF16)
    lo = (a - hi.astype(F32)).astype(BF16)
    return hi, lo


def _parts(a, n):
    return _split(a) if n == 2 else (a.astype(BF16),)


def _cat(parts_list, axis):
    return tuple(jnp.concatenate(xs, axis=axis) for xs in zip(*parts_list))


def _mmp(a, b, dims=None):
    acc = _dot(a[0], b[0], dims)
    if len(b) == 2:
        acc = acc + _dot(a[0], b[1], dims)
    if len(a) == 2:
        acc = acc + _dot(a[1], b[0], dims)
    return acc


def _dot2x(a, b_exact):
    return _dot(a, b_exact)


def _pair_ones():
    r = lax.broadcasted_iota(jnp.int32, (LANES, LANES), 0) // RW_HEAD
    c = lax.broadcasted_iota(jnp.int32, (LANES, LANES), 1) // RW_HEAD
    return jnp.where(r == c, 1.0, 0.0).astype(BF16)


def _segsum(x, ones):
    cols = [_dot2x(x[:, p * LANES:(p + 1) * LANES], ones) for p in range(x.shape[1] // LANES)]
    return jnp.concatenate(cols, axis=1) if len(cols) > 1 else cols[0]


def _blockdot(x, w_ref):
    cols = [_dot(x[:, p * LANES:(p + 1) * LANES], w_ref[p]) for p in range(N_PAIR)]
    return jnp.concatenate(cols, axis=1)


def _rwkv_elem(ps, w, ones):
    r = ps[:, 0:1024]
    k = ps[:, 1024:2048]
    v = ps[:, 2048:3072]
    slab = ps[:, 3072:3200]
    xg = ps[:, 3200:3328]
    ld = -DECAY_SCALE * _sigmoid(w['w0'][...] + _dot(jnp.tanh(slab), w['w2'][...]))
    a_sig = _sigmoid(w['a0'][...] + _dot(slab, w['a2'][...]))
    g = _dot(_sigmoid(xg), w['g2'][...])
    kk = k * w['k_k'][...]
    kk = kk * lax.rsqrt(jnp.maximum(_segsum(kk * kk, ones), 1e-24))
    k_mod = k * (1.0 + (a_sig - 1.0) * w['k_a'][...])
    return r, k_mod, v, ld, -kk, kk * a_sig, g


def _lru_elem(xc, y_in, w):
    gr = _sigmoid(_blockdot(xc, w['wa']) + w['ba'][...])
    gi = _sigmoid(_blockdot(xc, w['wx']) + w['bx'][...])
    log_a = -LRU_C * _softplus(-w['lam'][...]) * gr
    a = jnp.exp(log_a)
    bterm = jnp.sqrt(1.0 - jnp.exp(2.0 * log_a)) * (gi * xc)
    return a, bterm, _gelu_tanh(y_in)


_RW_NAMES = ('mu', 'w0', 'w2', 'a0', 'a2', 'g2', 'k_k', 'k_a', 'r_k', 'ln_g', 'ln_b')
_LRU_NAMES = ('cw', 'cb', 'wa', 'ba', 'wx', 'bx', 'lam')


def _stack(x, m0):
    return jnp.concatenate([jnp.where(m0, x, 0.0), jnp.where(m0, 0.0, x)], axis=0)


def _mixer_prompt_kernel(*refs):
    n_w = len(_RW_NAMES) + len(_LRU_NAMES)
    prw_ref, puy_ref = refs[0], refs[1]
    w = dict(zip(_RW_NAMES + _LRU_NAMES, refs[2:2 + n_w]))
    yrw_ref, ylru_ref, sout_ref, hout_ref = refs[2 + n_w:6 + n_w]
    pbuf, ubuf, s_scr, h_scr = refs[6 + n_w:]
    C = CHUNK
    c = pl.program_id(1)

    @pl.when(c == 0)
    def _():
        pbuf[0:8, :] = jnp.zeros((8, N_SHIFT), F32)
        ubuf[0:8, :] = jnp.zeros((8, D_MODEL), F32)
        s_scr[...] = jnp.zeros_like(s_scr)
        h_scr[...] = jnp.zeros_like(h_scr)

    ones = _pair_ones()
    row = lax.broadcasted_iota(jnp.int32, (C, 1), 0)

    lru_groups = list(range(N_PAIR))

    def lru_next():
        if not lru_groups:
            return
        hp = lru_groups.pop(0)
        sl = slice(hp * LANES, (hp + 1) * LANES)
        u_in = puy_ref[:, sl]
        ubuf[8:8 + C, sl] = u_in
        cw = w['cw']
        xc = (w['cb'][:, sl] + ubuf[5:5 + C, sl] * cw[0:1, sl] + ubuf[6:6 + C, sl] * cw[1:2, sl]
              + ubuf[7:7 + C, sl] * cw[2:3, sl] + u_in * cw[3:4, sl])
        ubuf[0:8, sl] = u_in[C - 8:C, :]
        gr = _sigmoid(_dot(xc, w['wa'][hp]) + w['ba'][:, sl])
        gi = _sigmoid(_dot(xc, w['wx'][hp]) + w['bx'][:, sl])
        log_a = -LRU_C * _softplus(-w['lam'][:, sl]) * gr
        a_l = jnp.exp(log_a)
        b_l = jnp.sqrt(1.0 - jnp.exp(2.0 * log_a)) * (gi * xc)
        d = 1
        while d < C:
            keep = row >= d
            a_sh = jnp.where(keep, pltpu.roll(a_l, d, 0), 1.0)
            b_sh = jnp.where(keep, pltpu.roll(b_l, d, 0), 0.0)
            b_l = a_l * b_sh + b_l
            a_l = a_l * a_sh
            d *= 2
        h = a_l * h_scr[0:1, sl] + b_l
        h_scr[0:1, sl] = h[C - 1:C, :]
        ylru_ref[:, sl] = h * _gelu_tanh(puy_ref[:, D_MODEL + hp * LANES:D_MODEL + (hp + 1) * LANES])

    p = prw_ref[...]
    pbuf[8:8 + C, :] = p
    prev = pbuf[7:7 + C, :]
    ps = p + w['mu'][...] * (prev - p)
    r, k_mod, v, ld, a_vec, b_vec, g = _rwkv_elem(ps, w, ones)
    pbuf[0:8, :] = p[C - 8:C, :]

    tri = jnp.where(lax.broadcasted_iota(jnp.int32, (C, C), 0) >= lax.broadcasted_iota(jnp.int32, (C, C), 1),
                    1.0, 0.0).astype(BF16)
    l_hi = ld.astype(BF16)
    l_r1 = ld - l_hi.astype(F32)
    l_mid = l_r1.astype(BF16)
    l_lo = (l_r1 - l_mid.astype(F32)).astype(BF16)
    cum = _dot(tri, l_hi) + (_dot(tri, l_mid) + _dot(tri, l_lo))
    cum_last = cum[C - 1:C, :]
    e_neg = jnp.exp(-cum)
    at = a_vec * jnp.exp(cum - ld)
    rt = r * jnp.exp(cum)
    bt = b_vec * e_neg
    kt = k_mod * e_neg
    wc = jnp.exp(cum_last)
    bh = bt * wc
    kh = kt * wc
    bonus_dot = _segsum(r * k_mod * w['r_k'][...], ones)

    n = 2 * C
    ri = lax.broadcasted_iota(jnp.int32, (n, n), 0)
    ci = lax.broadcasted_iota(jnp.int32, (n, n), 1)
    low_strict = ri > ci
    low_incl = ri >= ci
    diag_blk = (ri // SUB) == (ci // SUB)
    eye = jnp.where(ri == ci, 1.0, 0.0)
    m0 = lax.broadcasted_iota(jnp.int32, (C, LANES), 1) < RW_HEAD

    pairs = range(N_PAIR)
    sls = [slice(hp * LANES, (hp + 1) * LANES) for hp in pairs]
    pa, pr = SCAN_PARTS, OUT_PARTS

    def wide(xs, ys_list, px=pa, py=pa):
        outs = [[None] * N_PAIR for _ in ys_list]
        for p in pairs:
            o = _mmp(_parts(xs[p], px), _cat([_parts(ys[p], py) for ys in ys_list], 1))
            for i in range(len(ys_list)):
                outs[i][p] = o[:, i * n:(i + 1) * n]
        return outs

    vs_ = [_stack(v[:, sl], m0) for sl in sls]
    s_bd = [s_scr[hp] for hp in pairs]
    rhs = [_cat([_parts(_stack(bt[:, sl], m0), pa), _parts(_stack(kt[:, sl], m0), pa), _parts(s_bd[hp], pa)], 0)
           for hp, sl in zip(pairs, sls)]
    if pa == pr:
        gram = [_mmp(_cat([_parts(at[:, sl], pa), _parts(rt[:, sl], pr)], 0), rhs[hp], _NT)
                for hp, sl in zip(pairs, sls)]
    else:
        gram = [jnp.concatenate([_mmp(_parts(at[:, sl], pa), rhs[hp], _NT),
                                 _mmp(_parts(rt[:, sl], pr), rhs[hp][:pr], _NT)], axis=0)
                for hp, sl in zip(pairs, sls)]
    lru_next()
    n_ab = [jnp.where(low_strict, _stack(gm[0:C, 0:n], m0), 0.0) for gm in gram]
    a_ak = [jnp.where(low_strict, _stack(gm[0:C, n:2 * n], m0), 0.0) for gm in gram]
    a_s = [_stack(gm[0:C, 2 * n:3 * n], m0) for gm in gram]
    r_bk = [jnp.concatenate([jnp.where(low_incl, _stack(gm[C:n, 0:n], m0), 0.0),
                             jnp.where(low_incl, _stack(gm[C:n, n:2 * n], m0), 0.0)], axis=1) for gm in gram]
    r_s = [_stack(gm[C:n, 2 * n:3 * n], m0) for gm in gram]
    rhs_u = [x + y for x, y in zip(a_s, wide(a_ak, [vs_])[0])]
    lru_next()

    n_d = [jnp.where(diag_blk, x, 0.0) for x in n_ab]
    n_o = [x - d for x, d in zip(n_ab, n_d)]
    d_inv = [eye + d for d in n_d]
    q = wide(n_d, [n_d])[0]
    lru_next()
    for _ in range(2):
        q, qd = wide(q, [q, d_inv])
        d_inv = [d + x for d, x in zip(d_inv, qd)]
        lru_next()
    d_inv = [d + x for d, x in zip(d_inv, wide(q, [d_inv])[0])]
    lru_next()
    m1, z = wide(d_inv, [n_o, rhs_u])
    lru_next()
    m2, mz = wide(m1, [m1, z])
    lru_next()
    z = [x + y for x, y in zip(z, mz)]
    u = [x + y for x, y in zip(z, wide(m2, [z])[0])]
    uv = [jnp.concatenate([x, vv], axis=0) for x, vv in zip(u, vs_)]
    y_s = [rs + _mmp(_parts(rb, pr), _parts(x, pr)) for rs, rb, x in zip(r_s, r_bk, uv)]
    lru_next()
    s_new = [s_bd[hp] * wc[:, sl]
             + _mmp(_parts(uv[hp], pa),
                    _parts(jnp.concatenate([_stack(bh[:, sl], m0), _stack(kh[:, sl], m0)], axis=0), pa), _TN)
             for hp, sl in zip(pairs, sls)]
    for hp in pairs:
        s_scr[hp] = s_new[hp]
    while lru_groups:
        lru_next()

    y = [x[0:C] + x[C:n] for x in y_s]
    mu = [_dot2x(x, ones) * (1.0 / RW_HEAD) for x in y]
    yc = [x - m for x, m in zip(y, mu)]
    var = [_dot2x(x * x, ones) * (1.0 / RW_HEAD) for x in yc]
    for hp, sl in zip(pairs, sls):
        yn = yc[hp] * lax.rsqrt(var[hp] + GN_EPS) * w['ln_g'][:, sl] + w['ln_b'][:, sl]
        yrw_ref[:, sl] = (yn + bonus_dot[:, sl] * v[:, sl]) * g[:, sl]

    @pl.when(c == pl.num_programs(1) - 1)
    def _():
        hout_ref[0] = h_scr[0:1, :]
        for hp in range(N_PAIR):
            s_bd = s_scr[hp]
            sout_ref[0, 2 * hp] = s_bd[0:RW_HEAD, 0:RW_HEAD]
            sout_ref[0, 2 * hp + 1] = s_bd[RW_HEAD:LANES, RW_HEAD:LANES]


def _mixer_prompt(p_rw, p_uy, weights, batch, seq):
    C = CHUNK
    nc = seq // C
    row_map = lambda b, c: (b * nc + c, 0)
    w_specs = [pl.BlockSpec(x.shape, (lambda b, c, nd=x.ndim: (0,) * nd)) for x in weights]
    return pl.pallas_call(
        _mixer_prompt_kernel,
        grid=(batch, nc),
        in_specs=[pl.BlockSpec((C, N_SHIFT), row_map), pl.BlockSpec((C, C_UY), row_map)] + w_specs,
        out_specs=[pl.BlockSpec((C, D_MODEL), row_map), pl.BlockSpec((C, D_MODEL), row_map),
                   pl.BlockSpec((1, RW_HEADS, RW_HEAD, RW_HEAD), lambda b, c: (b, 0, 0, 0)),
                   pl.BlockSpec((1, 1, D_MODEL), lambda b, c: (b, 0, 0))],
        out_shape=[jax.ShapeDtypeStruct((batch * seq, D_MODEL), F32),
                   jax.ShapeDtypeStruct((batch * seq, D_MODEL), F32),
                   jax.ShapeDtypeStruct((batch, RW_HEADS, RW_HEAD, RW_HEAD), F32),
                   jax.ShapeDtypeStruct((batch, 1, D_MODEL), F32)],
        scratch_shapes=[pltpu.VMEM((8 + C, N_SHIFT), F32), pltpu.VMEM((8 + C, D_MODEL), F32),
                        pltpu.VMEM((N_PAIR, LANES, LANES), F32), pltpu.VMEM((8, D_MODEL), F32)],
        compiler_params=_params(("parallel", "arbitrary")),
        name="mixer_prompt",
    )(p_rw, p_uy, *weights)


def _mixer_sample_prep_kernel(*refs):
    n_w = len(_RW_NAMES) + len(_LRU_NAMES)
    prw_ref, shift_ref, puy_ref, conv_ref, h0_ref = refs[0:5]
    w = dict(zip(_RW_NAMES + _LRU_NAMES, refs[5:5 + n_w]))
    rw_ref, ylru_ref, hnew_ref = refs[5 + n_w:]
    ones = _pair_ones()
    p = prw_ref[...]
    ps = p + w['mu'][...] * (shift_ref[...] - p)
    r, k_mod, v, ld, a_vec, b_vec, g = _rwkv_elem(ps, w, ones)
    for i, x in enumerate((r, k_mod, v, jnp.exp(ld), a_vec, b_vec, g)):
        rw_ref[i] = x
    u_in = puy_ref[:, 0:D_MODEL]
    cw = w['cw']
    xc = (w['cb'][...] + conv_ref[:, 0:D_MODEL] * cw[0:1, :] + conv_ref[:, D_MODEL:2 * D_MODEL] * cw[1:2, :]
          + conv_ref[:, 2 * D_MODEL:3 * D_MODEL] * cw[2:3, :] + u_in * cw[3:4, :])
    a_l, b_l, gel = _lru_elem(xc, puy_ref[:, D_MODEL:2 * D_MODEL], w)
    h = a_l * h0_ref[...] + b_l
    hnew_ref[...] = h
    ylru_ref[...] = h * gel


def _mixer_sample_prep(p_rw, shift, p_uy, conv, h0, weights):
    nb = p_rw.shape[0]
    ins = [p_rw, shift, p_uy, conv, h0] + list(weights)
    full = lambda x: pl.BlockSpec(x.shape, (lambda i, nd=x.ndim: (0,) * nd))
    return pl.pallas_call(
        _mixer_sample_prep_kernel,
        grid=(1,),
        in_specs=[full(x) for x in ins],
        out_specs=[pl.BlockSpec((7, nb, D_MODEL), lambda i: (0, 0, 0)),
                   pl.BlockSpec((nb, D_MODEL), lambda i: (0, 0)),
                   pl.BlockSpec((nb, D_MODEL), lambda i: (0, 0))],
        out_shape=[jax.ShapeDtypeStruct((7, nb, D_MODEL), F32),
                   jax.ShapeDtypeStruct((nb, D_MODEL), F32),
                   jax.ShapeDtypeStruct((nb, D_MODEL), F32)],
        compiler_params=_params(("arbitrary",)),
        name="mixer_sample_prep",
    )(*ins)


def _rwkv_step_kernel(*refs, n_carry):
    if n_carry:
        s_ref, x_ref, rk_ref, lng_ref, lnb_ref, carry_ref, sout_ref, y_ref = refs
        sout_ref[0:n_carry] = carry_ref[...]
    else:
        s_ref, x_ref, rk_ref, lng_ref, lnb_ref, sout_ref, y_ref = refs
    s = s_ref[0]
    r, k, v, wd, a, b, g = (x_ref[i, 0] for i in range(7))
    sa = jnp.sum(s * a[None, :, :], axis=1)
    s_new = s * wd[None, :, :] + sa[:, None, :] * b[None, :, :] + v[:, None, :] * k[None, :, :]
    sout_ref[n_carry, 0] = s_new
    y = jnp.sum(s_new * r[None, :, :], axis=1)
    mu = jnp.mean(y, axis=0, keepdims=True)
    yc = y - mu
    var = jnp.mean(yc * yc, axis=0, keepdims=True)
    yn = yc * lax.rsqrt(var + GN_EPS) * lng_ref[0] + lnb_ref[0]
    bonus = jnp.sum(r * k * rk_ref[0], axis=0, keepdims=True) * v
    y_ref[0] = (yn + bonus) * g


def _rwkv_step(state, l, rw7, r_k, ln_g, ln_b, carry):
    n = state.shape[-1]
    blk = (RW_HEAD, RW_HEAD, n)
    head = lambda *lead: pl.BlockSpec(lead + (RW_HEAD, 1), lambda i: (i, 0, 0))
    in_specs = [pl.BlockSpec((None, 1) + blk, lambda i: (l, i, 0, 0, 0)),
                pl.BlockSpec((7, 1, RW_HEAD, n), lambda i: (0, i, 0, 0)),
                head(1), head(1), head(1)]
    args = [state, rw7, r_k, ln_g, ln_b]
    if carry is not None:
        in_specs.append(pl.BlockSpec((l, 1) + blk, lambda i: (0, i, 0, 0, 0)))
        args.append(carry)
    return pl.pallas_call(
        functools.partial(_rwkv_step_kernel, n_carry=0 if carry is None else l),
        grid=(RW_HEADS,),
        in_specs=in_specs,
        out_specs=[pl.BlockSpec((l + 1, 1) + blk, lambda i: (0, i, 0, 0, 0)),
                   pl.BlockSpec((1, RW_HEAD, n), lambda i: (i, 0, 0))],
        out_shape=[jax.ShapeDtypeStruct((l + 1, RW_HEADS) + blk, F32),
                   jax.ShapeDtypeStruct((RW_HEADS, RW_HEAD, n), F32)],
        compiler_params=_params(("parallel",)),
        name="rwkv_step",
    )(*args)


def _mm_kernel(x_ref, w_ref, o_ref):
    o_ref[...] = _dot(x_ref[...], w_ref[...])


def _mm(x, w, l, tm, tn, name):
    m, kd = x.shape
    n = w.shape[2]
    return pl.pallas_call(
        _mm_kernel,
        grid=(m // tm, n // tn),
        in_specs=[pl.BlockSpec((tm, kd), lambda i, j: (i, 0)),
                  pl.BlockSpec((None, kd, tn), lambda i, j: (l, 0, j))],
        out_specs=pl.BlockSpec((tm, tn), lambda i, j: (i, j)),
        out_shape=jax.ShapeDtypeStruct((m, n), F32),
        compiler_params=_params(("parallel", "parallel")),
        name=name,
    )(x, w)


def _proj3_kernel(x_ref, wrw_ref, wuy_ref, wg_ref, prw_ref, puy_ref, pg_ref):
    x = x_ref[...].astype(BF16)
    prw_ref[...] = _dot(x, wrw_ref[...])
    puy_ref[...] = _dot(x, wuy_ref[...])
    pg_ref[...] = _dot(x, wg_ref[...])


def _proj3(x, l, w_rw, w_uy, w_gate, tm):
    m = x.shape[0]
    rows = lambda width: pl.BlockSpec((tm, width), lambda i: (i, 0))
    widths = (w_rw.shape[2], w_uy.shape[2], w_gate.shape[2])
    return pl.pallas_call(
        _proj3_kernel,
        grid=(m // tm,),
        in_specs=[rows(D_MODEL), _layer_spec(w_rw, l), _layer_spec(w_uy, l), _layer_spec(w_gate, l)],
        out_specs=[rows(n) for n in widths],
        out_shape=[jax.ShapeDtypeStruct((m, n), F32) for n in widths],
        compiler_params=_params(("parallel",)),
        name="proj_in",
    )(x, w_rw, w_uy, w_gate)


def _post_kernel(x_ref, yrw_ref, ylru_ref, pg_ref, wrw_ref, wlru_ref, wmix_ref, gb_ref, g_ref, b_ref, wq_ref,
                 o_ref, q_ref):
    o_rw = _dot(yrw_ref[...], wrw_ref[...])
    o_lru = _dot(ylru_ref[...], wlru_ref[...])
    gates = _sigmoid(pg_ref[...] + gb_ref[...])
    mix = _dot(gates[:, 0:D_MODEL] * o_rw + gates[:, D_MODEL:2 * D_MODEL] * o_lru, wmix_ref[...])
    x1 = _layer_norm(ALPHA * x_ref[...] + mix, g_ref[...], b_ref[...])
    o_ref[...] = x1
    q_ref[...] = _dot(x1, wq_ref[...])


def _layer_spec(a, l):
    return pl.BlockSpec((None,) + a.shape[1:], lambda *_: (l, 0, 0))


def _post(x, y_rw, y_lru, p_gate, l, w_rw, w_lru, w_mix, gate_b, g, b, wq, tm):
    m = x.shape[0]
    rows = lambda width: pl.BlockSpec((tm, width), lambda i: (i, 0))
    const = lambda a: _layer_spec(a, l)
    return pl.pallas_call(
        _post_kernel,
        grid=(m // tm,),
        in_specs=[rows(D_MODEL), rows(D_MODEL), rows(D_MODEL), rows(2 * D_MODEL),
                  const(w_rw), const(w_lru), const(w_mix), const(gate_b), const(g), const(b), const(wq)],
        out_specs=[rows(D_MODEL), rows(D_MODEL)],
        out_shape=[jax.ShapeDtypeStruct((m, D_MODEL), F32), jax.ShapeDtypeStruct((m, D_MODEL), F32)],
        compiler_params=_params(("parallel",)),
        name="post",
    )(x, y_rw, y_lru, p_gate, w_rw, w_lru, w_mix, gate_b, g, b, wq)


def _attn_kernel(q_ref, k_ref, v_ref, o_ref):
    heads = [slice(h * XA_HEAD, (h + 1) * XA_HEAD) for h in range(XA_HEADS)]
    s = [_dot(q_ref[0, :, sl], k_ref[0, :, sl], _NT) * (XA_HEAD ** -0.5) for sl in heads]
    e = [jnp.exp(x - jnp.max(x, -1, keepdims=True)) for x in s]
    prob = [x / jnp.sum(x, -1, keepdims=True) for x in e]
    for sl, x in zip(heads, prob):
        o_ref[0, :, sl] = _dot(x, v_ref[0, :, sl])


def _attn(q, k, v, tq):
    b, t, _ = q.shape
    return pl.pallas_call(
        _attn_kernel,
        grid=(b, t // tq),
        in_specs=[pl.BlockSpec((1, tq, D_MODEL), lambda i, j: (i, j, 0)),
                  pl.BlockSpec((1, N_MEM, D_MODEL), lambda i, j: (i, 0, 0)),
                  pl.BlockSpec((1, N_MEM, D_MODEL), lambda i, j: (i, 0, 0))],
        out_specs=pl.BlockSpec((1, tq, D_MODEL), lambda i, j: (i, j, 0)),
        out_shape=jax.ShapeDtypeStruct(q.shape, F32),
        compiler_params=_params(("parallel", "parallel")),
        name="attn",
    )(q, k, v)


def _attn_decode_kernel(q_ref, k_ref, v_ref, o_ref):
    q = q_ref[...] * (XA_HEAD ** -0.5)
    s = jnp.sum(k_ref[...] * q[:, None, :, :], axis=-1, keepdims=True)
    e = jnp.exp(s - jnp.max(s, axis=1, keepdims=True))
    o_ref[...] = jnp.sum(e * v_ref[...], axis=1) / jnp.sum(e, axis=1)


def _attn_decode(q, k, v, l, nb):
    n = q.shape[0]
    kv = pl.BlockSpec((None, nb, N_MEM, XA_HEADS, XA_HEAD), lambda i: (l, i, 0, 0, 0))
    qo = pl.BlockSpec((nb, XA_HEADS, XA_HEAD), lambda i: (i, 0, 0))
    return pl.pallas_call(
        _attn_decode_kernel,
        grid=(n // nb,),
        in_specs=[qo, kv, kv],
        out_specs=qo,
        out_shape=jax.ShapeDtypeStruct(q.shape, F32),
        compiler_params=_params(("parallel",)),
        name="attn_decode",
    )(q, k, v)


def _tail_kernel(a_ref, x_ref, wo_ref, g2_ref, b2_ref, up_ref, down_ref, g3_ref, b3_ref, o_ref):
    x2 = _layer_norm(ALPHA * x_ref[...] + _dot(a_ref[...], wo_ref[...]), g2_ref[...], b2_ref[...])
    hdn = jnp.square(jnp.maximum(_dot(x2, up_ref[...]), 0.0))
    o_ref[...] = _layer_norm(ALPHA * x2 + _dot(hdn, down_ref[...]), g3_ref[...], b3_ref[...])


def _tail(a, x, l, wo, g2, b2, up, down, g3, b3, tm):
    m = a.shape[0]
    rows = pl.BlockSpec((tm, D_MODEL), lambda i: (i, 0))
    const = lambda z: _layer_spec(z, l)
    return pl.pallas_call(
        _tail_kernel,
        grid=(m // tm,),
        in_specs=[rows, rows, const(wo), const(g2), const(b2), const(up), const(down), const(g3), const(b3)],
        out_specs=rows,
        out_shape=jax.ShapeDtypeStruct((m, D_MODEL), F32),
        compiler_params=_params(("parallel",)),
        name="tail",
    )(a, x, wo, g2, b2, up, down, g3, b3)


def _row(x):
    return x.reshape(1, -1)


def _pair_blocks(wb):
    z = jnp.zeros((N_PAIR, RW_HEAD, RW_HEAD), wb.dtype)
    even, odd = wb[0::2], wb[1::2]
    top = jnp.concatenate([even, z], axis=2)
    bot = jnp.concatenate([z, odd], axis=2)
    return jnp.concatenate([top, bot], axis=1).astype(BF16)


def _layer_weights(l, W):
    z64 = jnp.zeros((64, D_MODEL), F32)
    rw = [_row(W['mu_shift'][l]), _row(W['rw_w0'][l]),
          jnp.concatenate([W['rw_w2'][l], z64], 0).astype(BF16), _row(W['rw_a0'][l]),
          jnp.concatenate([z64, W['rw_a2'][l]], 0).astype(BF16), W['rw_g2'][l].astype(BF16),
          _row(W['rw_k_k'][l]), _row(W['rw_k_a'][l]), _row(W['rw_r_k'][l]),
          _row(W['rw_lnx_g'][l]), _row(W['rw_lnx_b'][l])]
    lru = [W['lru_conv_w'][l], _row(W['lru_conv_b'][l]), _pair_blocks(W['lru_wa'][l]), _row(W['lru_ba'][l]),
           _pair_blocks(W['lru_wx'][l]), _row(W['lru_bx'][l]), _row(W['lru_lambda'][l])]
    return rw + lru


def _dense_weights(W):
    vec = lambda a: a.reshape(DEPTH, 1, -1)
    w_in = W['w_in']
    return dict(
        w_rw=w_in[:, :, 0:N_SHIFT].astype(BF16), w_uy=w_in[:, :, N_SHIFT:N_SHIFT + C_UY].astype(BF16),
        w_gate=w_in[:, :, N_SHIFT + C_UY:].astype(BF16),
        rw_proj=W['rw_proj'].astype(BF16), lru_proj=W['lru_proj'].astype(BF16), w_mix=W['w_out_mix'].astype(BF16),
        gate_b=vec(W['mix_gate_b']), wq=W['xa_wq'].astype(BF16), wo=W['xa_wo'].astype(BF16),
        up=W['mlp_up'].astype(BF16), down=W['mlp_down'].astype(BF16),
        ln1_g=vec(W['ln1_g']), ln1_b=vec(W['ln1_b']), ln2_g=vec(W['ln2_g']), ln2_b=vec(W['ln2_b']),
        ln3_g=vec(W['ln3_g']), ln3_b=vec(W['ln3_b']))


def _trunk(x, mem_k, mem_v, states, W, P, tm, tq):
    batch, seq, _ = x.shape
    m = batch * seq
    x2 = x.reshape(m, D_MODEL)
    o_rw_st, o_shift_st, o_h_st, o_conv_st = [], [], [], []
    s_new_prev = None
    if states is not None:
        states = (jnp.transpose(states[0], (0, 2, 3, 4, 1)),) + tuple(states[1:])
    for l in range(DEPTH):
        mixw = _layer_weights(l, W)
        p_rw, p_uy, p_gate = _proj3(x2, l, P['w_rw'], P['w_uy'], P['w_gate'], min(tm, 256))
        if states is None:
            y_rw, y_lru, s_new, h_new = _mixer_prompt(p_rw, p_uy, mixw, batch, seq)
            h_new = h_new.reshape(batch, D_MODEL)
            conv_new = p_uy.reshape(batch, seq, C_UY)[:, seq - (CONV_W - 1):, 0:D_MODEL]
        else:
            s_rw, s_shift, s_h, s_conv = states
            rw7, y_lru, h_new = _mixer_sample_prep(
                p_rw, s_shift[l], p_uy, s_conv[l].reshape(batch, (CONV_W - 1) * D_MODEL), s_h[l], mixw)
            hd = (RW_HEADS, RW_HEAD, 1)
            rw7 = jnp.transpose(rw7, (0, 2, 1)).reshape(7, RW_HEADS, RW_HEAD, batch)
            s_new, y_rw = _rwkv_step(s_rw, l, rw7, W['rw_r_k'][l].reshape(hd), W['rw_lnx_g'][l].reshape(hd),
                                     W['rw_lnx_b'][l].reshape(hd), s_new_prev)
            s_new_prev = s_new
            y_rw = y_rw.reshape(D_MODEL, batch).T
            conv_new = jnp.concatenate([s_conv[l][:, 1:], p_uy[:, None, 0:D_MODEL]], axis=1)
        x2, q = _post(x2, y_rw, y_lru, p_gate, l, P['rw_proj'], P['lru_proj'], P['w_mix'], P['gate_b'],
                      P['ln1_g'], P['ln1_b'], P['wq'], min(tm, 256))
        if states is None:
            o = _attn(q.reshape(batch, seq, D_MODEL), mem_k[l], mem_v[l], tq)
        else:
            o = _attn_decode(q.reshape(batch, XA_HEADS, XA_HEAD), mem_k, mem_v, l, 4)
        x2 = _tail(o.reshape(m, D_MODEL), x2, l, P['wo'], P['ln2_g'], P['ln2_b'], P['up'], P['down'],
                   P['ln3_g'], P['ln3_b'], tm)
        o_rw_st.append(s_new)
        o_shift_st.append(p_rw.reshape(batch, seq, N_SHIFT)[:, seq - 1])
        o_h_st.append(h_new)
        o_conv_st.append(conv_new)
    rw_state = jnp.stack(o_rw_st, 0) if states is None else jnp.transpose(s_new_prev, (0, 4, 1, 2, 3))
    return x2.reshape(batch, seq, D_MODEL), (rw_state, jnp.stack(o_shift_st, 0),
                                             jnp.stack(o_h_st, 0), jnp.stack(o_conv_st, 0))


def kernel(x_prompt, x_sample, mem_prompt, cache_mem_k, cache_mem_v, state_rwkv, state_rwkv_shift, state_lru_h, state_lru_conv, w_in, mu_shift, rw_w0, rw_w2, rw_a0, rw_a2, rw_g2, rw_k_k, rw_k_a, rw_r_k, rw_lnx_g, rw_lnx_b, rw_proj, lru_conv_w, lru_conv_b, lru_wa, lru_ba, lru_wx, lru_bx, lru_lambda, lru_proj, mix_gate_b, w_out_mix, ln1_g, ln1_b, xa_wq, xa_wk, xa_wv, xa_wo, ln2_g, ln2_b, mlp_up, mlp_down, ln3_g, ln3_b):
    W = dict(w_in=w_in, mu_shift=mu_shift, rw_w0=rw_w0, rw_w2=rw_w2, rw_a0=rw_a0, rw_a2=rw_a2,
             rw_g2=rw_g2, rw_k_k=rw_k_k, rw_k_a=rw_k_a, rw_r_k=rw_r_k, rw_lnx_g=rw_lnx_g,
             rw_lnx_b=rw_lnx_b, rw_proj=rw_proj, lru_conv_w=lru_conv_w, lru_conv_b=lru_conv_b,
             lru_wa=lru_wa, lru_ba=lru_ba, lru_wx=lru_wx, lru_bx=lru_bx, lru_lambda=lru_lambda,
             lru_proj=lru_proj, mix_gate_b=mix_gate_b, w_out_mix=w_out_mix, ln1_g=ln1_g, ln1_b=ln1_b,
             xa_wq=xa_wq, xa_wo=xa_wo, ln2_g=ln2_g, ln2_b=ln2_b, mlp_up=mlp_up, mlp_down=mlp_down,
             ln3_g=ln3_g, ln3_b=ln3_b)
    bp, seq, _ = x_prompt.shape
    bs = x_sample.shape[0]
    mem2 = mem_prompt.reshape(bp * N_MEM, D_MODEL)
    tmem = min(512, bp * N_MEM)
    wk, wv = xa_wk.astype(BF16), xa_wv.astype(BF16)
    mem_k_p = jnp.stack([_mm(mem2, wk, l, tmem, 1024, "mem_k") for l in range(DEPTH)], 0)
    mem_v_p = jnp.stack([_mm(mem2, wv, l, tmem, 1024, "mem_v") for l in range(DEPTH)], 0)
    mem_k_p = mem_k_p.reshape(DEPTH, bp, N_MEM, D_MODEL)
    mem_v_p = mem_v_p.reshape(DEPTH, bp, N_MEM, D_MODEL)
    P = _dense_weights(W)
    tm_p = min(512, bp * seq)
    y_prompt, (p_rw, p_shift, p_h, p_conv) = _trunk(x_prompt, mem_k_p, mem_v_p, None, W, P, tm_p, min(512, seq))
    y_sample, (s_rw, s_shift, s_h, s_conv) = _trunk(
        x_sample, cache_mem_k, cache_mem_v, (state_rwkv, state_rwkv_shift, state_lru_h, state_lru_conv),
        W, P, bs, 1)
    kv_shape = (DEPTH, bp, N_MEM, XA_HEADS, XA_HEAD)
    return (y_prompt, y_sample, p_rw, p_shift, p_h, p_conv, mem_k_p.reshape(kv_shape), mem_v_p.reshape(kv_shape),
            s_rw, s_shift, s_h, s_conv)
```

```python
import functools

import jax
import jax.numpy as jnp
from jax import lax
from jax.experimental import pallas as pl
from jax.experimental.pallas import tpu as pltpu

F32 = jnp.float32
BF16 = jnp.bfloat16

D_MODEL = 1024
DEPTH = 2
RW_HEAD = 64
RW_HEADS = 16
N_PAIR = 8
LANES = 128
R_GATE = 128
GN_EPS = 64e-5
LRU_C = 8.0
CONV_W = 4
N_MEM = 256
XA_HEADS = 4
XA_HEAD = 256
D_FF = 4096
ALPHA = (2 * DEPTH) ** 0.25
LN_EPS = 1e-5
N_SHIFT = 3328
C_UY = 2048
CHUNK = 64
SUB = 16
SCAN_PARTS = 1
OUT_PARTS = 1
DECAY_SCALE = 0.6065306597126334
VMEM_LIMIT = 56 * 1024 * 1024

_NT = (((1,), (1,)), ((), ()))
_TN = (((0,), (0,)), ((), ()))


def _params(sem):
    return pltpu.CompilerParams(dimension_semantics=sem, vmem_limit_bytes=VMEM_LIMIT)


def _sigmoid(x):
    return 1.0 / (1.0 + jnp.exp(-x))


def _softplus(x):
    return jnp.maximum(x, 0.0) + jnp.log(1.0 + jnp.exp(-jnp.abs(x)))


def _gelu_tanh(x):
    return 0.5 * x * (1.0 + jnp.tanh(0.7978845608028654 * (x + 0.044715 * (x * x * x))))


def _layer_norm(z, g, b):
    mu = jnp.mean(z, -1, keepdims=True)
    zc = z - mu
    var = jnp.mean(zc * zc, -1, keepdims=True)
    return zc * lax.rsqrt(var + LN_EPS) * g + b


def _dot(a, b, dims=None):
    a = a.astype(BF16)
    b = b.astype(BF16)
    if dims is None:
        return jnp.dot(a, b, preferred_element_type=F32)
    return lax.dot_general(a, b, dims, preferred_element_type=F32)


def _split(a):
    hi = a.astype(BF16)
    lo = (a - hi.astype(F32)).astype(BF16)
    return hi, lo


def _parts(a, n):
    return _split(a) if n == 2 else (a.astype(BF16),)


def _cat(parts_list, axis):
    return tuple(jnp.concatenate(xs, axis=axis) for xs in zip(*parts_list))


def _mmp(a, b, dims=None):
    acc = _dot(a[0], b[0], dims)
    if len(b) == 2:
        acc = acc + _dot(a[0], b[1], dims)
    if len(a) == 2:
        acc = acc + _dot(a[1], b[0], dims)
    return acc


def _dot2x(a, b_exact):
    return _dot(a, b_exact)


def _pair_ones():
    r = lax.broadcasted_iota(jnp.int32, (LANES, LANES), 0) // RW_HEAD
    c = lax.broadcasted_iota(jnp.int32, (LANES, LANES), 1) // RW_HEAD
    return jnp.where(r == c, 1.0, 0.0).astype(BF16)


def _segsum(x, ones):
    cols = [_dot2x(x[:, p * LANES:(p + 1) * LANES], ones) for p in range(x.shape[1] // LANES)]
    return jnp.concatenate(cols, axis=1) if len(cols) > 1 else cols[0]


def _blockdot(x, w_ref):
    cols = [_dot(x[:, p * LANES:(p + 1) * LANES], w_ref[p]) for p in range(N_PAIR)]
    return jnp.concatenate(cols, axis=1)


def _rwkv_elem(ps, w, ones):
    r = ps[:, 0:1024]
    k = ps[:, 1024:2048]
    v = ps[:, 2048:3072]
    slab = ps[:, 3072:3200]
    xg = ps[:, 3200:3328]
    ld = -DECAY_SCALE * _sigmoid(w['w0'][...] + _dot(jnp.tanh(slab), w['w2'][...]))
    a_sig = _sigmoid(w['a0'][...] + _dot(slab, w['a2'][...]))
    g = _dot(_sigmoid(xg), w['g2'][...])
    kk = k * w['k_k'][...]
    kk = kk * lax.rsqrt(jnp.maximum(_segsum(kk * kk, ones), 1e-24))
    k_mod = k * (1.0 + (a_sig - 1.0) * w['k_a'][...])
    return r, k_mod, v, ld, -kk, kk * a_sig, g


def _lru_elem(xc, y_in, w):
    gr = _sigmoid(_blockdot(xc, w['wa']) + w['ba'][...])
    gi = _sigmoid(_blockdot(xc, w['wx']) + w['bx'][...])
    log_a = -LRU_C * _softplus(-w['lam'][...]) * gr
    a = jnp.exp(log_a)
    bterm = jnp.sqrt(1.0 - jnp.exp(2.0 * log_a)) * (gi * xc)
    return a, bterm, _gelu_tanh(y_in)


_RW_NAMES = ('mu', 'w0', 'w2', 'a0', 'a2', 'g2', 'k_k', 'k_a', 'r_k', 'ln_g', 'ln_b')
_LRU_NAMES = ('cw', 'cb', 'wa', 'ba', 'wx', 'bx', 'lam')


def _stack(x, m0):
    return jnp.concatenate([jnp.where(m0, x, 0.0), jnp.where(m0, 0.0, x)], axis=0)


def _mixer_prompt_kernel(*refs):
    n_w = len(_RW_NAMES) + len(_LRU_NAMES)
    prw_ref, puy_ref = refs[0], refs[1]
    w = dict(zip(_RW_NAMES + _LRU_NAMES, refs[2:2 + n_w]))
    yrw_ref, ylru_ref, sout_ref, hout_ref = refs[2 + n_w:6 + n_w]
    pbuf, ubuf, s_scr, h_scr = refs[6 + n_w:]
    C = CHUNK
    c = pl.program_id(1)

    @pl.when(c == 0)
    def _():
        pbuf[0:8, :] = jnp.zeros((8, N_SHIFT), F32)
        ubuf[0:8, :] = jnp.zeros((8, D_MODEL), F32)
        s_scr[...] = jnp.zeros_like(s_scr)
        h_scr[...] = jnp.zeros_like(h_scr)

    ones = _pair_ones()
    row = lax.broadcasted_iota(jnp.int32, (C, 1), 0)

    lru_groups = list(range(N_PAIR))

    def lru_next():
        if not lru_groups:
            return
        hp = lru_groups.pop(0)
        sl = slice(hp * LANES, (hp + 1) * LANES)
        u_in = puy_ref[:, sl]
        ubuf[8:8 + C, sl] = u_in
        cw = w['cw']
        xc = (w['cb'][:, sl] + ubuf[5:5 + C, sl] * cw[0:1, sl] + ubuf[6:6 + C, sl] * cw[1:2, sl]
              + ubuf[7:7 + C, sl] * cw[2:3, sl] + u_in * cw[3:4, sl])
        ubuf[0:8, sl] = u_in[C - 8:C, :]
        gr = _sigmoid(_dot(xc, w['wa'][hp]) + w['ba'][:, sl])
        gi = _sigmoid(_dot(xc, w['wx'][hp]) + w['bx'][:, sl])
        log_a = -LRU_C * _softplus(-w['lam'][:, sl]) * gr
        a_l = jnp.exp(log_a)
        b_l = jnp.sqrt(1.0 - jnp.exp(2.0 * log_a)) * (gi * xc)
        d = 1
        while d < C:
            keep = row >= d
            a_sh = jnp.where(keep, pltpu.roll(a_l, d, 0), 1.0)
            b_sh = jnp.where(keep, pltpu.roll(b_l, d, 0), 0.0)
            b_l = a_l * b_sh + b_l
            a_l = a_l * a_sh
            d *= 2
        h = a_l * h_scr[0:1, sl] + b_l
        h_scr[0:1, sl] = h[C - 1:C, :]
        y_lru = h * _gelu_tanh(puy_ref[:, D_MODEL + hp * LANES:D_MODEL + (hp + 1) * LANES])
        ylru_ref[:, sl] = y_lru.astype(ylru_ref.dtype)

    p = prw_ref[...]
    pbuf[8:8 + C, :] = p
    prev = pbuf[7:7 + C, :]
    ps = p + w['mu'][...] * (prev - p)
    r, k_mod, v, ld, a_vec, b_vec, g = _rwkv_elem(ps, w, ones)
    pbuf[0:8, :] = p[C - 8:C, :]

    tri = jnp.where(lax.broadcasted_iota(jnp.int32, (C, C), 0) >= lax.broadcasted_iota(jnp.int32, (C, C), 1),
                    1.0, 0.0).astype(BF16)
    l_hi = ld.astype(BF16)
    l_r1 = ld - l_hi.astype(F32)
    l_mid = l_r1.astype(BF16)
    l_lo = (l_r1 - l_mid.astype(F32)).astype(BF16)
    cum = _dot(tri, l_hi) + (_dot(tri, l_mid) + _dot(tri, l_lo))
    cum_last = cum[C - 1:C, :]
    e_neg = jnp.exp(-cum)
    at = a_vec * jnp.exp(cum - ld)
    rt = r * jnp.exp(cum)
    bt = b_vec * e_neg
    kt = k_mod * e_neg
    wc = jnp.exp(cum_last)
    bh = bt * wc
    kh = kt * wc
    bonus_dot = _segsum(r * k_mod * w['r_k'][...], ones)

    n = 2 * C
    ri = lax.broadcasted_iota(jnp.int32, (n, n), 0)
    ci = lax.broadcasted_iota(jnp.int32, (n, n), 1)
    low_strict = ri > ci
    low_incl = ri >= ci
    diag_blk = (ri // SUB) == (ci // SUB)
    eye = jnp.where(ri == ci, 1.0, 0.0)
    m0 = lax.broadcasted_iota(jnp.int32, (C, LANES), 1) < RW_HEAD

    pairs = range(N_PAIR)
    sls = [slice(hp * LANES, (hp + 1) * LANES) for hp in pairs]
    pa, pr = SCAN_PARTS, OUT_PARTS

    def wide(xs, ys_list, px=pa, py=pa):
        outs = [[None] * N_PAIR for _ in ys_list]
        for p in pairs:
            o = _mmp(_parts(xs[p], px), _cat([_parts(ys[p], py) for ys in ys_list], 1))
            for i in range(len(ys_list)):
                outs[i][p] = o[:, i * n:(i + 1) * n]
        return outs

    vs_ = [_stack(v[:, sl], m0) for sl in sls]
    s_bd = [s_scr[hp] for hp in pairs]
    rhs = [_cat([_parts(_stack(bt[:, sl], m0), pa), _parts(_stack(kt[:, sl], m0), pa), _parts(s_bd[hp], pa)], 0)
           for hp, sl in zip(pairs, sls)]
    if pa == pr:
        gram = [_mmp(_cat([_parts(at[:, sl], pa), _parts(rt[:, sl], pr)], 0), rhs[hp], _NT)
                for hp, sl in zip(pairs, sls)]
    else:
        gram = [jnp.concatenate([_mmp(_parts(at[:, sl], pa), rhs[hp], _NT),
                                 _mmp(_parts(rt[:, sl], pr), rhs[hp][:pr], _NT)], axis=0)
                for hp, sl in zip(pairs, sls)]
    lru_next()
    n_ab = [jnp.where(low_strict, _stack(gm[0:C, 0:n], m0), 0.0) for gm in gram]
    a_ak = [jnp.where(low_strict, _stack(gm[0:C, n:2 * n], m0), 0.0) for gm in gram]
    a_s = [_stack(gm[0:C, 2 * n:3 * n], m0) for gm in gram]
    r_bk = [jnp.concatenate([jnp.where(low_incl, _stack(gm[C:n, 0:n], m0), 0.0),
                             jnp.where(low_incl, _stack(gm[C:n, n:2 * n], m0), 0.0)], axis=1) for gm in gram]
    r_s = [_stack(gm[C:n, 2 * n:3 * n], m0) for gm in gram]
    rhs_u = [x + y for x, y in zip(a_s, wide(a_ak, [vs_])[0])]
    lru_next()

    n_d = [jnp.where(diag_blk, x, 0.0) for x in n_ab]
    n_o = [x - d for x, d in zip(n_ab, n_d)]
    d_inv = [eye + d for d in n_d]
    q = wide(n_d, [n_d])[0]
    lru_next()
    for _ in range(2):
        q, qd = wide(q, [q, d_inv])
        d_inv = [d + x for d, x in zip(d_inv, qd)]
        lru_next()
    d_inv = [d + x for d, x in zip(d_inv, wide(q, [d_inv])[0])]
    lru_next()
    m1, z = wide(d_inv, [n_o, rhs_u])
    lru_next()
    m2, mz = wide(m1, [m1, z])
    lru_next()
    z = [x + y for x, y in zip(z, mz)]
    u = [x + y for x, y in zip(z, wide(m2, [z])[0])]
    uv = [jnp.concatenate([x, vv], axis=0) for x, vv in zip(u, vs_)]
    y_s = [rs + _mmp(_parts(rb, pr), _parts(x, pr)) for rs, rb, x in zip(r_s, r_bk, uv)]
    lru_next()
    s_new = [s_bd[hp] * wc[:, sl]
             + _mmp(_parts(uv[hp], pa),
                    _parts(jnp.concatenate([_stack(bh[:, sl], m0), _stack(kh[:, sl], m0)], axis=0), pa), _TN)
             for hp, sl in zip(pairs, sls)]
    for hp in pairs:
        s_scr[hp] = s_new[hp]
    while lru_groups:
        lru_next()

    y = [x[0:C] + x[C:n] for x in y_s]
    mu = [_dot2x(x, ones) * (1.0 / RW_HEAD) for x in y]
    yc = [x - m for x, m in zip(y, mu)]
    var = [_dot2x(x * x, ones) * (1.0 / RW_HEAD) for x in yc]
    for hp, sl in zip(pairs, sls):
        yn = yc[hp] * lax.rsqrt(var[hp] + GN_EPS) * w['ln_g'][:, sl] + w['ln_b'][:, sl]
        yrw_ref[:, sl] = ((yn + bonus_dot[:, sl] * v[:, sl]) * g[:, sl]).astype(yrw_ref.dtype)

    @pl.when(c == pl.num_programs(1) - 1)
    def _():
        hout_ref[0] = h_scr[0:1, :]
        for hp in range(N_PAIR):
            s_bd = s_scr[hp]
            sout_ref[0, 2 * hp] = s_bd[0:RW_HEAD, 0:RW_HEAD]
            sout_ref[0, 2 * hp + 1] = s_bd[RW_HEAD:LANES, RW_HEAD:LANES]


def _mixer_prompt(p_rw, p_uy, weights, batch, seq):
    C = CHUNK
    nc = seq // C
    row_map = lambda b, c: (b * nc + c, 0)
    w_specs = [pl.BlockSpec(x.shape, (lambda b, c, nd=x.ndim: (0,) * nd)) for x in weights]
    return pl.pallas_call(
        _mixer_prompt_kernel,
        grid=(batch, nc),
        in_specs=[pl.BlockSpec((C, N_SHIFT), row_map), pl.BlockSpec((C, C_UY), row_map)] + w_specs,
        out_specs=[pl.BlockSpec((C, D_MODEL), row_map), pl.BlockSpec((C, D_MODEL), row_map),
                   pl.BlockSpec((1, RW_HEADS, RW_HEAD, RW_HEAD), lambda b, c: (b, 0, 0, 0)),
                   pl.BlockSpec((1, 1, D_MODEL), lambda b, c: (b, 0, 0))],
        out_shape=[jax.ShapeDtypeStruct((batch * seq, D_MODEL), BF16),
                   jax.ShapeDtypeStruct((batch * seq, D_MODEL), BF16),
                   jax.ShapeDtypeStruct((batch, RW_HEADS, RW_HEAD, RW_HEAD), F32),
                   jax.ShapeDtypeStruct((batch, 1, D_MODEL), F32)],
        scratch_shapes=[pltpu.VMEM((8 + C, N_SHIFT), F32), pltpu.VMEM((8 + C, D_MODEL), F32),
                        pltpu.VMEM((N_PAIR, LANES, LANES), F32), pltpu.VMEM((8, D_MODEL), F32)],
        compiler_params=_params(("parallel", "arbitrary")),
        name="mixer_prompt",
    )(p_rw, p_uy, *weights)


def _mixer_sample_prep_kernel(*refs):
    n_w = len(_RW_NAMES) + len(_LRU_NAMES)
    prw_ref, shift_ref, puy_ref, conv_ref, h0_ref = refs[0:5]
    w = dict(zip(_RW_NAMES + _LRU_NAMES, refs[5:5 + n_w]))
    rw_ref, ylru_ref, hnew_ref = refs[5 + n_w:]
    ones = _pair_ones()
    p = prw_ref[...]
    ps = p + w['mu'][...] * (shift_ref[...] - p)
    r, k_mod, v, ld, a_vec, b_vec, g = _rwkv_elem(ps, w, ones)
    for i, x in enumerate((r, k_mod, v, jnp.exp(ld), a_vec, b_vec, g)):
        rw_ref[i] = x
    u_in = puy_ref[:, 0:D_MODEL]
    cw = w['cw']
    xc = (w['cb'][...] + conv_ref[:, 0:D_MODEL] * cw[0:1, :] + conv_ref[:, D_MODEL:2 * D_MODEL] * cw[1:2, :]
          + conv_ref[:, 2 * D_MODEL:3 * D_MODEL] * cw[2:3, :] + u_in * cw[3:4, :])
    a_l, b_l, gel = _lru_elem(xc, puy_ref[:, D_MODEL:2 * D_MODEL], w)
    h = a_l * h0_ref[...] + b_l
    hnew_ref[...] = h
    ylru_ref[...] = h * gel


def _mixer_sample_prep(p_rw, shift, p_uy, conv, h0, weights):
    nb = p_rw.shape[0]
    ins = [p_rw, shift, p_uy, conv, h0] + list(weights)
    full = lambda x: pl.BlockSpec(x.shape, (lambda i, nd=x.ndim: (0,) * nd))
    return pl.pallas_call(
        _mixer_sample_prep_kernel,
        grid=(1,),
        in_specs=[full(x) for x in ins],
        out_specs=[pl.BlockSpec((7, nb, D_MODEL), lambda i: (0, 0, 0)),
                   pl.BlockSpec((nb, D_MODEL), lambda i: (0, 0)),
                   pl.BlockSpec((nb, D_MODEL), lambda i: (0, 0))],
        out_shape=[jax.ShapeDtypeStruct((7, nb, D_MODEL), F32),
                   jax.ShapeDtypeStruct((nb, D_MODEL), F32),
                   jax.ShapeDtypeStruct((nb, D_MODEL), F32)],
        compiler_params=_params(("arbitrary",)),
        name="mixer_sample_prep",
    )(*ins)


def _rwkv_step_kernel(*refs, n_carry):
    if n_carry:
        s_ref, x_ref, rk_ref, lng_ref, lnb_ref, carry_ref, sout_ref, y_ref = refs
        sout_ref[0:n_carry] = carry_ref[...]
    else:
        s_ref, x_ref, rk_ref, lng_ref, lnb_ref, sout_ref, y_ref = refs
    s = s_ref[0]
    r, k, v, wd, a, b, g = (x_ref[i, 0] for i in range(7))
    sa = jnp.sum(s * a[None, :, :], axis=1)
    s_new = s * wd[None, :, :] + sa[:, None, :] * b[None, :, :] + v[:, None, :] * k[None, :, :]
    sout_ref[n_carry, 0] = s_new
    y = jnp.sum(s_new * r[None, :, :], axis=1)
    mu = jnp.mean(y, axis=0, keepdims=True)
    yc = y - mu
    var = jnp.mean(yc * yc, axis=0, keepdims=True)
    yn = yc * lax.rsqrt(var + GN_EPS) * lng_ref[0] + lnb_ref[0]
    bonus = jnp.sum(r * k * rk_ref[0], axis=0, keepdims=True) * v
    y_ref[0] = (yn + bonus) * g


def _rwkv_step(state, l, rw7, r_k, ln_g, ln_b, carry):
    n = state.shape[-1]
    blk = (RW_HEAD, RW_HEAD, n)
    head = lambda *lead: pl.BlockSpec(lead + (RW_HEAD, 1), lambda i: (i, 0, 0))
    in_specs = [pl.BlockSpec((None, 1) + blk, lambda i: (l, i, 0, 0, 0)),
                pl.BlockSpec((7, 1, RW_HEAD, n), lambda i: (0, i, 0, 0)),
                head(1), head(1), head(1)]
    args = [state, rw7, r_k, ln_g, ln_b]
    if carry is not None:
        in_specs.append(pl.BlockSpec((l, 1) + blk, lambda i: (0, i, 0, 0, 0)))
        args.append(carry)
    return pl.pallas_call(
        functools.partial(_rwkv_step_kernel, n_carry=0 if carry is None else l),
        grid=(RW_HEADS,),
        in_specs=in_specs,
        out_specs=[pl.BlockSpec((l + 1, 1) + blk, lambda i: (0, i, 0, 0, 0)),
                   pl.BlockSpec((1, RW_HEAD, n), lambda i: (i, 0, 0))],
        out_shape=[jax.ShapeDtypeStruct((l + 1, RW_HEADS) + blk, F32),
                   jax.ShapeDtypeStruct((RW_HEADS, RW_HEAD, n), F32)],
        compiler_params=_params(("parallel",)),
        name="rwkv_step",
    )(*args)


def _mm_kernel(x_ref, w_ref, o_ref):
    o_ref[...] = _dot(x_ref[...], w_ref[...])


def _mm(x, w, l, tm, tn, name):
    m, kd = x.shape
    n = w.shape[2]
    return pl.pallas_call(
        _mm_kernel,
        grid=(m // tm, n // tn),
        in_specs=[pl.BlockSpec((tm, kd), lambda i, j: (i, 0)),
                  pl.BlockSpec((None, kd, tn), lambda i, j: (l, 0, j))],
        out_specs=pl.BlockSpec((tm, tn), lambda i, j: (i, j)),
        out_shape=jax.ShapeDtypeStruct((m, n), F32),
        compiler_params=_params(("parallel", "parallel")),
        name=name,
    )(x, w)


def _proj3_kernel(x_ref, wrw_ref, wuy_ref, wg_ref, prw_ref, puy_ref, pg_ref):
    x = x_ref[...].astype(BF16)
    prw_ref[...] = _dot(x, wrw_ref[...])
    puy_ref[...] = _dot(x, wuy_ref[...])
    pg_ref[...] = _dot(x, wg_ref[...])


def _proj3(x, l, w_rw, w_uy, w_gate, tm):
    m = x.shape[0]
    rows = lambda width: pl.BlockSpec((tm, width), lambda i: (i, 0))
    widths = (w_rw.shape[2], w_uy.shape[2], w_gate.shape[2])
    return pl.pallas_call(
        _proj3_kernel,
        grid=(m // tm,),
        in_specs=[rows(D_MODEL), _layer_spec(w_rw, l), _layer_spec(w_uy, l), _layer_spec(w_gate, l)],
        out_specs=[rows(n) for n in widths],
        out_shape=[jax.ShapeDtypeStruct((m, n), F32) for n in widths],
        compiler_params=_params(("parallel",)),
        name="proj_in",
    )(x, w_rw, w_uy, w_gate)


def _post_kernel(x_ref, yrw_ref, ylru_ref, pg_ref, wrw_ref, wlru_ref, wmix_ref, gb_ref, g_ref, b_ref, wq_ref,
                 o_ref, q_ref):
    o_rw = _dot(yrw_ref[...], wrw_ref[...])
    o_lru = _dot(ylru_ref[...], wlru_ref[...])
    gates = _sigmoid(pg_ref[...] + gb_ref[...])
    mix = _dot(gates[:, 0:D_MODEL] * o_rw + gates[:, D_MODEL:2 * D_MODEL] * o_lru, wmix_ref[...])
    x1 = _layer_norm(ALPHA * x_ref[...] + mix, g_ref[...], b_ref[...])
    o_ref[...] = x1
    q_ref[...] = _dot(x1, wq_ref[...]).astype(q_ref.dtype)


def _layer_spec(a, l):
    return pl.BlockSpec((None,) + a.shape[1:], lambda *_: (l, 0, 0))


def _post(x, y_rw, y_lru, p_gate, l, w_rw, w_lru, w_mix, gate_b, g, b, wq, tm, q_dtype):
    m = x.shape[0]
    rows = lambda width: pl.BlockSpec((tm, width), lambda i: (i, 0))
    const = lambda a: _layer_spec(a, l)
    return pl.pallas_call(
        _post_kernel,
        grid=(m // tm,),
        in_specs=[rows(D_MODEL), rows(D_MODEL), rows(D_MODEL), rows(2 * D_MODEL),
                  const(w_rw), const(w_lru), const(w_mix), const(gate_b), const(g), const(b), const(wq)],
        out_specs=[rows(D_MODEL), rows(D_MODEL)],
        out_shape=[jax.ShapeDtypeStruct((m, D_MODEL), F32), jax.ShapeDtypeStruct((m, D_MODEL), q_dtype)],
        compiler_params=_params(("parallel",)),
        name="post",
    )(x, y_rw, y_lru, p_gate, w_rw, w_lru, w_mix, gate_b, g, b, wq)


def _attn_kernel(q_ref, k_ref, v_ref, o_ref):
    heads = [slice(h * XA_HEAD, (h + 1) * XA_HEAD) for h in range(XA_HEADS)]
    s = [_dot(q_ref[0, :, sl], k_ref[0, :, sl], _NT) * (XA_HEAD ** -0.5) for sl in heads]
    e = [jnp.exp(x - jnp.max(x, -1, keepdims=True)) for x in s]
    prob = [x / jnp.sum(x, -1, keepdims=True) for x in e]
    for sl, x in zip(heads, prob):
        o_ref[0, :, sl] = _dot(x, v_ref[0, :, sl]).astype(o_ref.dtype)


def _attn(q, k, v, tq):
    b, t, _ = q.shape
    return pl.pallas_call(
        _attn_kernel,
        grid=(b, t // tq),
        in_specs=[pl.BlockSpec((1, tq, D_MODEL), lambda i, j: (i, j, 0)),
                  pl.BlockSpec((1, N_MEM, D_MODEL), lambda i, j: (i, 0, 0)),
                  pl.BlockSpec((1, N_MEM, D_MODEL), lambda i, j: (i, 0, 0))],
        out_specs=pl.BlockSpec((1, tq, D_MODEL), lambda i, j: (i, j, 0)),
        out_shape=jax.ShapeDtypeStruct(q.shape, BF16),
        compiler_params=_params(("parallel", "parallel")),
        name="attn",
    )(q, k, v)


def _attn_decode_kernel(q_ref, k_ref, v_ref, o_ref):
    q = q_ref[...] * (XA_HEAD ** -0.5)
    s = jnp.sum(k_ref[...] * q[:, None, :, :], axis=-1, keepdims=True)
    e = jnp.exp(s - jnp.max(s, axis=1, keepdims=True))
    o_ref[...] = jnp.sum(e * v_ref[...], axis=1) / jnp.sum(e, axis=1)


def _attn_decode(q, k, v, l, nb):
    n = q.shape[0]
    kv = pl.BlockSpec((None, nb, N_MEM, XA_HEADS, XA_HEAD), lambda i: (l, i, 0, 0, 0))
    qo = pl.BlockSpec((nb, XA_HEADS, XA_HEAD), lambda i: (i, 0, 0))
    return pl.pallas_call(
        _attn_decode_kernel,
        grid=(n // nb,),
        in_specs=[qo, kv, kv],
        out_specs=qo,
        out_shape=jax.ShapeDtypeStruct(q.shape, F32),
        compiler_params=_params(("parallel",)),
        name="attn_decode",
    )(q, k, v)


def _tail_kernel(a_ref, x_ref, wo_ref, g2_ref, b2_ref, up_ref, down_ref, g3_ref, b3_ref, o_ref):
    x2 = _layer_norm(ALPHA * x_ref[...] + _dot(a_ref[...], wo_ref[...]), g2_ref[...], b2_ref[...])
    hdn = jnp.square(jnp.maximum(_dot(x2, up_ref[...]), 0.0))
    o_ref[...] = _layer_norm(ALPHA * x2 + _dot(hdn, down_ref[...]), g3_ref[...], b3_ref[...])


def _tail(a, x, l, wo, g2, b2, up, down, g3, b3, tm):
    m = a.shape[0]
    rows = pl.BlockSpec((tm, D_MODEL), lambda i: (i, 0))
    const = lambda z: _layer_spec(z, l)
    return pl.pallas_call(
        _tail_kernel,
        grid=(m // tm,),
        in_specs=[rows, rows, const(wo), const(g2), const(b2), const(up), const(down), const(g3), const(b3)],
        out_specs=rows,
        out_shape=jax.ShapeDtypeStruct((m, D_MODEL), F32),
        compiler_params=_params(("parallel",)),
        name="tail",
    )(a, x, wo, g2, b2, up, down, g3, b3)


def _row(x):
    return x.reshape(1, -1)


def _pair_blocks(wb):
    z = jnp.zeros((N_PAIR, RW_HEAD, RW_HEAD), wb.dtype)
    even, odd = wb[0::2], wb[1::2]
    top = jnp.concatenate([even, z], axis=2)
    bot = jnp.concatenate([z, odd], axis=2)
    return jnp.concatenate([top, bot], axis=1).astype(BF16)


def _layer_weights(l, W):
    z64 = jnp.zeros((64, D_MODEL), F32)
    rw = [_row(W['mu_shift'][l]), _row(W['rw_w0'][l]),
          jnp.concatenate([W['rw_w2'][l], z64], 0).astype(BF16), _row(W['rw_a0'][l]),
          jnp.concatenate([z64, W['rw_a2'][l]], 0).astype(BF16), W['rw_g2'][l].astype(BF16),
          _row(W['rw_k_k'][l]), _row(W['rw_k_a'][l]), _row(W['rw_r_k'][l]),
          _row(W['rw_lnx_g'][l]), _row(W['rw_lnx_b'][l])]
    lru = [W['lru_conv_w'][l], _row(W['lru_conv_b'][l]), _pair_blocks(W['lru_wa'][l]), _row(W['lru_ba'][l]),
           _pair_blocks(W['lru_wx'][l]), _row(W['lru_bx'][l]), _row(W['lru_lambda'][l])]
    return rw + lru


def _dense_weights(W):
    vec = lambda a: a.reshape(DEPTH, 1, -1)
    w_in = W['w_in']
    return dict(
        w_rw=w_in[:, :, 0:N_SHIFT].astype(BF16), w_uy=w_in[:, :, N_SHIFT:N_SHIFT + C_UY].astype(BF16),
        w_gate=w_in[:, :, N_SHIFT + C_UY:].astype(BF16),
        rw_proj=W['rw_proj'].astype(BF16), lru_proj=W['lru_proj'].astype(BF16), w_mix=W['w_out_mix'].astype(BF16),
        gate_b=vec(W['mix_gate_b']), wq=W['xa_wq'].astype(BF16), wo=W['xa_wo'].astype(BF16),
        up=W['mlp_up'].astype(BF16), down=W['mlp_down'].astype(BF16),
        ln1_g=vec(W['ln1_g']), ln1_b=vec(W['ln1_b']), ln2_g=vec(W['ln2_g']), ln2_b=vec(W['ln2_b']),
        ln3_g=vec(W['ln3_g']), ln3_b=vec(W['ln3_b']))


def _trunk(x, mem_k, mem_v, states, W, P, tm, tq):
    batch, seq, _ = x.shape
    m = batch * seq
    x2 = x.reshape(m, D_MODEL)
    o_rw_st, o_shift_st, o_h_st, o_conv_st = [], [], [], []
    s_new_prev = None
    if states is not None:
        states = (jnp.transpose(states[0], (0, 2, 3, 4, 1)),) + tuple(states[1:])
    for l in range(DEPTH):
        mixw = _layer_weights(l, W)
        p_rw, p_uy, p_gate = _proj3(x2, l, P['w_rw'], P['w_uy'], P['w_gate'], min(tm, 256))
        if states is None:
            y_rw, y_lru, s_new, h_new = _mixer_prompt(p_rw, p_uy, mixw, batch, seq)
            h_new = h_new.reshape(batch, D_MODEL)
            conv_new = p_uy.reshape(batch, seq, C_UY)[:, seq - (CONV_W - 1):, 0:D_MODEL]
        else:
            s_rw, s_shift, s_h, s_conv = states
            rw7, y_lru, h_new = _mixer_sample_prep(
                p_rw, s_shift[l], p_uy, s_conv[l].reshape(batch, (CONV_W - 1) * D_MODEL), s_h[l], mixw)
            hd = (RW_HEADS, RW_HEAD, 1)
            rw7 = jnp.transpose(rw7, (0, 2, 1)).reshape(7, RW_HEADS, RW_HEAD, batch)
            s_new, y_rw = _rwkv_step(s_rw, l, rw7, W['rw_r_k'][l].reshape(hd), W['rw_lnx_g'][l].reshape(hd),
                                     W['rw_lnx_b'][l].reshape(hd), s_new_prev)
            s_new_prev = s_new
            y_rw = y_rw.reshape(D_MODEL, batch).T
            conv_new = jnp.concatenate([s_conv[l][:, 1:], p_uy[:, None, 0:D_MODEL]], axis=1)
        x2, q = _post(x2, y_rw, y_lru, p_gate, l, P['rw_proj'], P['lru_proj'], P['w_mix'], P['gate_b'],
                      P['ln1_g'], P['ln1_b'], P['wq'], min(tm, 512), BF16 if states is None else F32)
        if states is None:
            o = _attn(q.reshape(batch, seq, D_MODEL), mem_k[l], mem_v[l], tq)
        else:
            o = _attn_decode(q.reshape(batch, XA_HEADS, XA_HEAD), mem_k, mem_v, l, 4)
        x2 = _tail(o.reshape(m, D_MODEL), x2, l, P['wo'], P['ln2_g'], P['ln2_b'], P['up'], P['down'],
                   P['ln3_g'], P['ln3_b'], tm)
        o_rw_st.append(s_new)
        o_shift_st.append(p_rw.reshape(batch, seq, N_SHIFT)[:, seq - 1])
        o_h_st.append(h_new)
        o_conv_st.append(conv_new)
    rw_state = jnp.stack(o_rw_st, 0) if states is None else jnp.transpose(s_new_prev, (0, 4, 1, 2, 3))
    return x2.reshape(batch, seq, D_MODEL), (rw_state, jnp.stack(o_shift_st, 0),
                                             jnp.stack(o_h_st, 0), jnp.stack(o_conv_st, 0))


def kernel(x_prompt, x_sample, mem_prompt, cache_mem_k, cache_mem_v, state_rwkv, state_rwkv_shift, state_lru_h, state_lru_conv, w_in, mu_shift, rw_w0, rw_w2, rw_a0, rw_a2, rw_g2, rw_k_k, rw_k_a, rw_r_k, rw_lnx_g, rw_lnx_b, rw_proj, lru_conv_w, lru_conv_b, lru_wa, lru_ba, lru_wx, lru_bx, lru_lambda, lru_proj, mix_gate_b, w_out_mix, ln1_g, ln1_b, xa_wq, xa_wk, xa_wv, xa_wo, ln2_g, ln2_b, mlp_up, mlp_down, ln3_g, ln3_b):
    W = dict(w_in=w_in, mu_shift=mu_shift, rw_w0=rw_w0, rw_w2=rw_w2, rw_a0=rw_a0, rw_a2=rw_a2,
             rw_g2=rw_g2, rw_k_k=rw_k_k, rw_k_a=rw_k_a, rw_r_k=rw_r_k, rw_lnx_g=rw_lnx_g,
             rw_lnx_b=rw_lnx_b, rw_proj=rw_proj, lru_conv_w=lru_conv_w, lru_conv_b=lru_conv_b,
             lru_wa=lru_wa, lru_ba=lru_ba, lru_wx=lru_wx, lru_bx=lru_bx, lru_lambda=lru_lambda,
             lru_proj=lru_proj, mix_gate_b=mix_gate_b, w_out_mix=w_out_mix, ln1_g=ln1_g, ln1_b=ln1_b,
             xa_wq=xa_wq, xa_wo=xa_wo, ln2_g=ln2_g, ln2_b=ln2_b, mlp_up=mlp_up, mlp_down=mlp_down,
             ln3_g=ln3_g, ln3_b=ln3_b)
    bp, seq, _ = x_prompt.shape
    bs = x_sample.shape[0]
    mem2 = mem_prompt.reshape(bp * N_MEM, D_MODEL)
    tmem = min(512, bp * N_MEM)
    wk, wv = xa_wk.astype(BF16), xa_wv.astype(BF16)
    mem_k_p = jnp.stack([_mm(mem2, wk, l, tmem, 1024, "mem_k") for l in range(DEPTH)], 0)
    mem_v_p = jnp.stack([_mm(mem2, wv, l, tmem, 1024, "mem_v") for l in range(DEPTH)], 0)
    mem_k_p = mem_k_p.reshape(DEPTH, bp, N_MEM, D_MODEL)
    mem_v_p = mem_v_p.reshape(DEPTH, bp, N_MEM, D_MODEL)
    P = _dense_weights(W)
    tm_p = min(512, bp * seq)
    y_prompt, (p_rw, p_shift, p_h, p_conv) = _trunk(x_prompt, mem_k_p, mem_v_p, None, W, P, tm_p, min(512, seq))
    y_sample, (s_rw, s_shift, s_h, s_conv) = _trunk(
        x_sample, cache_mem_k, cache_mem_v, (state_rwkv, state_rwkv_shift, state_lru_h, state_lru_conv),
        W, P, bs, 1)
    kv_shape = (DEPTH, bp, N_MEM, XA_HEADS, XA_HEAD)
    return (y_prompt, y_sample, p_rw, p_shift, p_h, p_conv, mem_k_p.reshape(kv_shape), mem_v_p.reshape(kv_shape),
            s_rw, s_shift, s_h, s_conv)
```

```python
import functools

import jax
import jax.numpy as jnp
from jax import lax
from jax.experimental import pallas as pl
from jax.experimental.pallas import tpu as pltpu

F32 = jnp.float32
BF16 = jnp.bfloat16

D_MODEL = 1024
DEPTH = 2
RW_HEAD = 64
RW_HEADS = 16
N_PAIR = 8
LANES = 128
R_GATE = 128
GN_EPS = 64e-5
LRU_C = 8.0
CONV_W = 4
N_MEM = 256
XA_HEADS = 4
XA_HEAD = 256
D_FF = 4096
ALPHA = (2 * DEPTH) ** 0.25
LN_EPS = 1e-5
N_SHIFT = 3328
C_UY = 2048
CHUNK = 64
SUB = 16
SCAN_PARTS = 1
OUT_PARTS = 1
DECAY_SCALE = 0.6065306597126334
VMEM_LIMIT = 56 * 1024 * 1024

_NT = (((1,), (1,)), ((), ()))
_TN = (((0,), (0,)), ((), ()))


def _params(sem):
    return pltpu.CompilerParams(dimension_semantics=sem, vmem_limit_bytes=VMEM_LIMIT)


def _sigmoid(x):
    return 1.0 / (1.0 + jnp.exp(-x))


def _softplus(x):
    return jnp.maximum(x, 0.0) + jnp.log(1.0 + jnp.exp(-jnp.abs(x)))


def _gelu_tanh(x):
    return 0.5 * x * (1.0 + jnp.tanh(0.7978845608028654 * (x + 0.044715 * (x * x * x))))


def _layer_norm(z, g, b):
    mu = jnp.mean(z, -1, keepdims=True)
    zc = z - mu
    var = jnp.mean(zc * zc, -1, keepdims=True)
    return zc * lax.rsqrt(var + LN_EPS) * g + b


def _dot(a, b, dims=None):
    a = a.astype(BF16)
    b = b.astype(BF16)
    if dims is None:
        return jnp.dot(a, b, preferred_element_type=F32)
    return lax.dot_general(a, b, dims, preferred_element_type=F32)


def _split(a):
    hi = a.astype(BF16)
    lo = (a - hi.astype(F32)).astype(BF16)
    return hi, lo


def _parts(a, n):
    return _split(a) if n == 2 else (a.astype(BF16),)


def _cat(parts_list, axis):
    return tuple(jnp.concatenate(xs, axis=axis) for xs in zip(*parts_list))


def _mmp(a, b, dims=None):
    acc = _dot(a[0], b[0], dims)
    if len(b) == 2:
        acc = acc + _dot(a[0], b[1], dims)
    if len(a) == 2:
        acc = acc + _dot(a[1], b[0], dims)
    return acc


def _dot2x(a, b_exact):
    return _dot(a, b_exact)


def _pair_ones():
    r = lax.broadcasted_iota(jnp.int32, (LANES, LANES), 0) // RW_HEAD
    c = lax.broadcasted_iota(jnp.int32, (LANES, LANES), 1) // RW_HEAD
    return jnp.where(r == c, 1.0, 0.0).astype(BF16)


def _segsum(x, ones):
    cols = [_dot2x(x[:, p * LANES:(p + 1) * LANES], ones) for p in range(x.shape[1] // LANES)]
    return jnp.concatenate(cols, axis=1) if len(cols) > 1 else cols[0]


def _blockdot(x, w_ref):
    cols = [_dot(x[:, p * LANES:(p + 1) * LANES], w_ref[p]) for p in range(N_PAIR)]
    return jnp.concatenate(cols, axis=1)


def _rwkv_elem(ps, w, ones):
    r = ps[:, 0:1024]
    k = ps[:, 1024:2048]
    v = ps[:, 2048:3072]
    slab = ps[:, 3072:3200]
    xg = ps[:, 3200:3328]
    ld = -DECAY_SCALE * _sigmoid(w['w0'][...] + _dot(jnp.tanh(slab), w['w2'][...]))
    a_sig = _sigmoid(w['a0'][...] + _dot(slab, w['a2'][...]))
    g = _dot(_sigmoid(xg), w['g2'][...])
    kk = k * w['k_k'][...]
    kk = kk * lax.rsqrt(jnp.maximum(_segsum(kk * kk, ones), 1e-24))
    k_mod = k * (1.0 + (a_sig - 1.0) * w['k_a'][...])
    return r, k_mod, v, ld, -kk, kk * a_sig, g


def _lru_elem(xc, y_in, w):
    gr = _sigmoid(_blockdot(xc, w['wa']) + w['ba'][...])
    gi = _sigmoid(_blockdot(xc, w['wx']) + w['bx'][...])
    log_a = -LRU_C * _softplus(-w['lam'][...]) * gr
    a = jnp.exp(log_a)
    bterm = jnp.sqrt(1.0 - jnp.exp(2.0 * log_a)) * (gi * xc)
    return a, bterm, _gelu_tanh(y_in)


_RW_NAMES = ('mu', 'w0', 'w2', 'a0', 'a2', 'g2', 'k_k', 'k_a', 'r_k', 'ln_g', 'ln_b')
_LRU_NAMES = ('cw', 'cb', 'wa', 'ba', 'wx', 'bx', 'lam')


def _stack(x, m0):
    return jnp.concatenate([jnp.where(m0, x, 0.0), jnp.where(m0, 0.0, x)], axis=0)


def _mixer_prompt_kernel(*refs):
    n_w = len(_RW_NAMES) + len(_LRU_NAMES)
    prw_ref, puy_ref = refs[0], refs[1]
    w = dict(zip(_RW_NAMES + _LRU_NAMES, refs[2:2 + n_w]))
    yrw_ref, ylru_ref, sout_ref, hout_ref = refs[2 + n_w:6 + n_w]
    pbuf, ubuf, s_scr, h_scr = refs[6 + n_w:]
    C = CHUNK
    c = pl.program_id(1)

    @pl.when(c == 0)
    def _():
        pbuf[0:8, :] = jnp.zeros((8, N_SHIFT), F32)
        ubuf[0:8, :] = jnp.zeros((8, D_MODEL), F32)
        s_scr[...] = jnp.zeros_like(s_scr)
        h_scr[...] = jnp.zeros_like(h_scr)

    ones = _pair_ones()
    row = lax.broadcasted_iota(jnp.int32, (C, 1), 0)

    lru_groups = list(range(N_PAIR))

    def lru_next():
        if not lru_groups:
            return
        hp = lru_groups.pop(0)
        sl = slice(hp * LANES, (hp + 1) * LANES)
        u_in = puy_ref[:, sl]
        ubuf[8:8 + C, sl] = u_in
        cw = w['cw']
        xc = (w['cb'][:, sl] + ubuf[5:5 + C, sl] * cw[0:1, sl] + ubuf[6:6 + C, sl] * cw[1:2, sl]
              + ubuf[7:7 + C, sl] * cw[2:3, sl] + u_in * cw[3:4, sl])
        ubuf[0:8, sl] = u_in[C - 8:C, :]
        gr = _sigmoid(_dot(xc, w['wa'][hp]) + w['ba'][:, sl])
        gi = _sigmoid(_dot(xc, w['wx'][hp]) + w['bx'][:, sl])
        log_a = -LRU_C * _softplus(-w['lam'][:, sl]) * gr
        a_l = jnp.exp(log_a)
        b_l = jnp.sqrt(1.0 - jnp.exp(2.0 * log_a)) * (gi * xc)
        d = 1
        while d < C:
            keep = row >= d
            a_sh = jnp.where(keep, pltpu.roll(a_l, d, 0), 1.0)
            b_sh = jnp.where(keep, pltpu.roll(b_l, d, 0), 0.0)
            b_l = a_l * b_sh + b_l
            a_l = a_l * a_sh
            d *= 2
        h = a_l * h_scr[0:1, sl] + b_l
        h_scr[0:1, sl] = h[C - 1:C, :]
        y_lru = h * _gelu_tanh(puy_ref[:, D_MODEL + hp * LANES:D_MODEL + (hp + 1) * LANES])
        ylru_ref[:, sl] = y_lru.astype(ylru_ref.dtype)

    p = prw_ref[...]
    pbuf[8:8 + C, :] = p
    prev = pbuf[7:7 + C, :]
    ps = p + w['mu'][...] * (prev - p)
    r, k_mod, v, ld, a_vec, b_vec, g = _rwkv_elem(ps, w, ones)
    pbuf[0:8, :] = p[C - 8:C, :]

    tri = jnp.where(lax.broadcasted_iota(jnp.int32, (C, C), 0) >= lax.broadcasted_iota(jnp.int32, (C, C), 1),
                    1.0, 0.0).astype(BF16)
    l_hi = ld.astype(BF16)
    l_r1 = ld - l_hi.astype(F32)
    cum = _dot(tri, l_hi) + _dot(tri, l_r1.astype(BF16))
    cum_last = cum[C - 1:C, :]
    e_neg = jnp.exp(-cum)
    at = a_vec * jnp.exp(cum - ld)
    rt = r * jnp.exp(cum)
    bt = b_vec * e_neg
    kt = k_mod * e_neg
    wc = jnp.exp(cum_last)
    bh = bt * wc
    kh = kt * wc
    bonus_dot = _segsum(r * k_mod * w['r_k'][...], ones)

    n = 2 * C
    ri = lax.broadcasted_iota(jnp.int32, (n, n), 0)
    ci = lax.broadcasted_iota(jnp.int32, (n, n), 1)
    low_strict = ri > ci
    low_incl = ri >= ci
    diag_blk = (ri // SUB) == (ci // SUB)
    eye = jnp.where(ri == ci, 1.0, 0.0)
    m0 = lax.broadcasted_iota(jnp.int32, (C, LANES), 1) < RW_HEAD

    pairs = range(N_PAIR)
    sls = [slice(hp * LANES, (hp + 1) * LANES) for hp in pairs]
    pa, pr = SCAN_PARTS, OUT_PARTS

    def wide(xs, ys_list, px=pa, py=pa):
        outs = [[None] * N_PAIR for _ in ys_list]
        for p in pairs:
            o = _mmp(_parts(xs[p], px), _cat([_parts(ys[p], py) for ys in ys_list], 1))
            for i in range(len(ys_list)):
                outs[i][p] = o[:, i * n:(i + 1) * n]
        return outs

    vs_ = [_stack(v[:, sl], m0) for sl in sls]
    s_bd = [s_scr[hp] for hp in pairs]
    rhs = [_cat([_parts(_stack(bt[:, sl], m0), pa), _parts(_stack(kt[:, sl], m0), pa), _parts(s_bd[hp], pa)], 0)
           for hp, sl in zip(pairs, sls)]
    if pa == pr:
        gram = [_mmp(_cat([_parts(at[:, sl], pa), _parts(rt[:, sl], pr)], 0), rhs[hp], _NT)
                for hp, sl in zip(pairs, sls)]
    else:
        gram = [jnp.concatenate([_mmp(_parts(at[:, sl], pa), rhs[hp], _NT),
                                 _mmp(_parts(rt[:, sl], pr), rhs[hp][:pr], _NT)], axis=0)
                for hp, sl in zip(pairs, sls)]
    lru_next()
    n_ab = [jnp.where(low_strict, _stack(gm[0:C, 0:n], m0), 0.0) for gm in gram]
    a_ak = [jnp.where(low_strict, _stack(gm[0:C, n:2 * n], m0), 0.0) for gm in gram]
    a_s = [_stack(gm[0:C, 2 * n:3 * n], m0) for gm in gram]
    r_bk = [jnp.concatenate([jnp.where(low_incl, _stack(gm[C:n, 0:n], m0), 0.0),
                             jnp.where(low_incl, _stack(gm[C:n, n:2 * n], m0), 0.0)], axis=1) for gm in gram]
    r_s = [_stack(gm[C:n, 2 * n:3 * n], m0) for gm in gram]
    rhs_u = [x + y for x, y in zip(a_s, wide(a_ak, [vs_])[0])]
    lru_next()

    n_d = [jnp.where(diag_blk, x, 0.0) for x in n_ab]
    n_o = [x - d for x, d in zip(n_ab, n_d)]
    d_inv = [eye + d for d in n_d]
    q = wide(n_d, [n_d])[0]
    lru_next()
    for _ in range(2):
        q, qd = wide(q, [q, d_inv])
        d_inv = [d + x for d, x in zip(d_inv, qd)]
        lru_next()
    d_inv = [d + x for d, x in zip(d_inv, wide(q, [d_inv])[0])]
    lru_next()
    m1, z = wide(d_inv, [n_o, rhs_u])
    lru_next()
    m2, mz = wide(m1, [m1, z])
    lru_next()
    z = [x + y for x, y in zip(z, mz)]
    u = [x + y for x, y in zip(z, wide(m2, [z])[0])]
    uv = [jnp.concatenate([x, vv], axis=0) for x, vv in zip(u, vs_)]
    y_s = [rs + _mmp(_parts(rb, pr), _parts(x, pr)) for rs, rb, x in zip(r_s, r_bk, uv)]
    lru_next()
    s_new = [s_bd[hp] * wc[:, sl]
             + _mmp(_parts(uv[hp], pa),
                    _parts(jnp.concatenate([_stack(bh[:, sl], m0), _stack(kh[:, sl], m0)], axis=0), pa), _TN)
             for hp, sl in zip(pairs, sls)]
    for hp in pairs:
        s_scr[hp] = s_new[hp]
    while lru_groups:
        lru_next()

    y = [x[0:C] + x[C:n] for x in y_s]
    mu = [_dot2x(x, ones) * (1.0 / RW_HEAD) for x in y]
    yc = [x - m for x, m in zip(y, mu)]
    var = [_dot2x(x * x, ones) * (1.0 / RW_HEAD) for x in yc]
    for hp, sl in zip(pairs, sls):
        yn = yc[hp] * lax.rsqrt(var[hp] + GN_EPS) * w['ln_g'][:, sl] + w['ln_b'][:, sl]
        yrw_ref[:, sl] = ((yn + bonus_dot[:, sl] * v[:, sl]) * g[:, sl]).astype(yrw_ref.dtype)

    @pl.when(c == pl.num_programs(1) - 1)
    def _():
        hout_ref[0] = h_scr[0:1, :]
        for hp in range(N_PAIR):
            s_bd = s_scr[hp]
            sout_ref[0, 2 * hp] = s_bd[0:RW_HEAD, 0:RW_HEAD]
            sout_ref[0, 2 * hp + 1] = s_bd[RW_HEAD:LANES, RW_HEAD:LANES]


def _mixer_prompt(p_rw, p_uy, weights, batch, seq):
    C = CHUNK
    nc = seq // C
    row_map = lambda b, c: (b * nc + c, 0)
    w_specs = [pl.BlockSpec(x.shape, (lambda b, c, nd=x.ndim: (0,) * nd)) for x in weights]
    return pl.pallas_call(
        _mixer_prompt_kernel,
        grid=(batch, nc),
        in_specs=[pl.BlockSpec((C, N_SHIFT), row_map), pl.BlockSpec((C, C_UY), row_map)] + w_specs,
        out_specs=[pl.BlockSpec((C, D_MODEL), row_map), pl.BlockSpec((C, D_MODEL), row_map),
                   pl.BlockSpec((1, RW_HEADS, RW_HEAD, RW_HEAD), lambda b, c: (b, 0, 0, 0)),
                   pl.BlockSpec((1, 1, D_MODEL), lambda b, c: (b, 0, 0))],
        out_shape=[jax.ShapeDtypeStruct((batch * seq, D_MODEL), BF16),
                   jax.ShapeDtypeStruct((batch * seq, D_MODEL), BF16),
                   jax.ShapeDtypeStruct((batch, RW_HEADS, RW_HEAD, RW_HEAD), F32),
                   jax.ShapeDtypeStruct((batch, 1, D_MODEL), F32)],
        scratch_shapes=[pltpu.VMEM((8 + C, N_SHIFT), F32), pltpu.VMEM((8 + C, D_MODEL), F32),
                        pltpu.VMEM((N_PAIR, LANES, LANES), F32), pltpu.VMEM((8, D_MODEL), F32)],
        compiler_params=_params(("parallel", "arbitrary")),
        name="mixer_prompt",
    )(p_rw, p_uy, *weights)


def _mixer_sample_prep_kernel(*refs):
    n_w = len(_RW_NAMES) + len(_LRU_NAMES)
    prw_ref, shift_ref, puy_ref, conv_ref, h0_ref = refs[0:5]
    w = dict(zip(_RW_NAMES + _LRU_NAMES, refs[5:5 + n_w]))
    rw_ref, ylru_ref, hnew_ref = refs[5 + n_w:]
    ones = _pair_ones()
    p = prw_ref[...]
    ps = p + w['mu'][...] * (shift_ref[...] - p)
    r, k_mod, v, ld, a_vec, b_vec, g = _rwkv_elem(ps, w, ones)
    for i, x in enumerate((r, k_mod, v, jnp.exp(ld), a_vec, b_vec, g)):
        rw_ref[i] = x
    u_in = puy_ref[:, 0:D_MODEL]
    cw = w['cw']
    xc = (w['cb'][...] + conv_ref[:, 0:D_MODEL] * cw[0:1, :] + conv_ref[:, D_MODEL:2 * D_MODEL] * cw[1:2, :]
          + conv_ref[:, 2 * D_MODEL:3 * D_MODEL] * cw[2:3, :] + u_in * cw[3:4, :])
    a_l, b_l, gel = _lru_elem(xc, puy_ref[:, D_MODEL:2 * D_MODEL], w)
    h = a_l * h0_ref[...] + b_l
    hnew_ref[...] = h
    ylru_ref[...] = h * gel


def _mixer_sample_prep(p_rw, shift, p_uy, conv, h0, weights):
    nb = p_rw.shape[0]
    ins = [p_rw, shift, p_uy, conv, h0] + list(weights)
    full = lambda x: pl.BlockSpec(x.shape, (lambda i, nd=x.ndim: (0,) * nd))
    return pl.pallas_call(
        _mixer_sample_prep_kernel,
        grid=(1,),
        in_specs=[full(x) for x in ins],
        out_specs=[pl.BlockSpec((7, nb, D_MODEL), lambda i: (0, 0, 0)),
                   pl.BlockSpec((nb, D_MODEL), lambda i: (0, 0)),
                   pl.BlockSpec((nb, D_MODEL), lambda i: (0, 0))],
        out_shape=[jax.ShapeDtypeStruct((7, nb, D_MODEL), F32),
                   jax.ShapeDtypeStruct((nb, D_MODEL), F32),
                   jax.ShapeDtypeStruct((nb, D_MODEL), F32)],
        compiler_params=_params(("arbitrary",)),
        name="mixer_sample_prep",
    )(*ins)


def _rwkv_step_kernel(*refs, n_carry):
    if n_carry:
        s_ref, x_ref, rk_ref, lng_ref, lnb_ref, carry_ref, sout_ref, y_ref = refs
        sout_ref[0:n_carry] = carry_ref[...]
    else:
        s_ref, x_ref, rk_ref, lng_ref, lnb_ref, sout_ref, y_ref = refs
    s = s_ref[0]
    r, k, v, wd, a, b, g = (x_ref[i, 0] for i in range(7))
    sa = jnp.sum(s * a[None, :, :], axis=1)
    s_new = s * wd[None, :, :] + sa[:, None, :] * b[None, :, :] + v[:, None, :] * k[None, :, :]
    sout_ref[n_carry, 0] = s_new
    y = jnp.sum(s_new * r[None, :, :], axis=1)
    mu = jnp.mean(y, axis=0, keepdims=True)
    yc = y - mu
    var = jnp.mean(yc * yc, axis=0, keepdims=True)
    yn = yc * lax.rsqrt(var + GN_EPS) * lng_ref[0] + lnb_ref[0]
    bonus = jnp.sum(r * k * rk_ref[0], axis=0, keepdims=True) * v
    y_ref[0] = (yn + bonus) * g


def _rwkv_step(state, l, rw7, r_k, ln_g, ln_b, carry):
    n = state.shape[-1]
    blk = (RW_HEAD, RW_HEAD, n)
    head = lambda *lead: pl.BlockSpec(lead + (RW_HEAD, 1), lambda i: (i, 0, 0))
    in_specs = [pl.BlockSpec((None, 1) + blk, lambda i: (l, i, 0, 0, 0)),
                pl.BlockSpec((7, 1, RW_HEAD, n), lambda i: (0, i, 0, 0)),
                head(1), head(1), head(1)]
    args = [state, rw7, r_k, ln_g, ln_b]
    if carry is not None:
        in_specs.append(pl.BlockSpec((l, 1) + blk, lambda i: (0, i, 0, 0, 0)))
        args.append(carry)
    return pl.pallas_call(
        functools.partial(_rwkv_step_kernel, n_carry=0 if carry is None else l),
        grid=(RW_HEADS,),
        in_specs=in_specs,
        out_specs=[pl.BlockSpec((l + 1, 1) + blk, lambda i: (0, i, 0, 0, 0)),
                   pl.BlockSpec((1, RW_HEAD, n), lambda i: (i, 0, 0))],
        out_shape=[jax.ShapeDtypeStruct((l + 1, RW_HEADS) + blk, F32),
                   jax.ShapeDtypeStruct((RW_HEADS, RW_HEAD, n), F32)],
        compiler_params=_params(("parallel",)),
        name="rwkv_step",
    )(*args)


def _mm_kernel(x_ref, w_ref, o_ref):
    o_ref[...] = _dot(x_ref[...], w_ref[...])


def _mm(x, w, l, tm, tn, name):
    m, kd = x.shape
    n = w.shape[2]
    return pl.pallas_call(
        _mm_kernel,
        grid=(m // tm, n // tn),
        in_specs=[pl.BlockSpec((tm, kd), lambda i, j: (i, 0)),
                  pl.BlockSpec((None, kd, tn), lambda i, j: (l, 0, j))],
        out_specs=pl.BlockSpec((tm, tn), lambda i, j: (i, j)),
        out_shape=jax.ShapeDtypeStruct((m, n), F32),
        compiler_params=_params(("parallel", "parallel")),
        name=name,
    )(x, w)


_PROJ_COLS = (0, N_SHIFT, N_SHIFT + C_UY, N_SHIFT + C_UY + 2 * D_MODEL)


def _proj3_kernel(x_ref, w_ref, prw_ref, puy_ref, pg_ref):
    x = x_ref[...].astype(BF16)
    for o_ref, c0, c1 in zip((prw_ref, puy_ref, pg_ref), _PROJ_COLS[:-1], _PROJ_COLS[1:]):
        o_ref[...] = _dot(x, w_ref[:, c0:c1])


def _proj3(x, l, w_in, tm):
    m = x.shape[0]
    rows = lambda width: pl.BlockSpec((tm, width), lambda i: (i, 0))
    widths = [c1 - c0 for c0, c1 in zip(_PROJ_COLS[:-1], _PROJ_COLS[1:])]
    return pl.pallas_call(
        _proj3_kernel,
        grid=(m // tm,),
        in_specs=[rows(D_MODEL), _layer_spec(w_in, l)],
        out_specs=[rows(n) for n in widths],
        out_shape=[jax.ShapeDtypeStruct((m, n), F32) for n in widths],
        compiler_params=_params(("parallel",)),
        name="proj_in",
    )(x, w_in)


def _post_kernel(x_ref, yrw_ref, ylru_ref, pg_ref, wrw_ref, wlru_ref, wmix_ref, gb_ref, g_ref, b_ref, wq_ref,
                 o_ref, q_ref):
    o_rw = _dot(yrw_ref[...], wrw_ref[...])
    o_lru = _dot(ylru_ref[...], wlru_ref[...])
    gates = _sigmoid(pg_ref[...] + gb_ref[...])
    mix = _dot(gates[:, 0:D_MODEL] * o_rw + gates[:, D_MODEL:2 * D_MODEL] * o_lru, wmix_ref[...])
    x1 = _layer_norm(ALPHA * x_ref[...] + mix, g_ref[...], b_ref[...])
    o_ref[...] = x1
    q_ref[...] = _dot(x1, wq_ref[...]).astype(q_ref.dtype)


def _layer_spec(a, l):
    return pl.BlockSpec((None,) + a.shape[1:], lambda *_: (l, 0, 0))


def _post(x, y_rw, y_lru, p_gate, l, w_rw, w_lru, w_mix, gate_b, g, b, wq, tm, q_dtype):
    m = x.shape[0]
    rows = lambda width: pl.BlockSpec((tm, width), lambda i: (i, 0))
    const = lambda a: _layer_spec(a, l)
    return pl.pallas_call(
        _post_kernel,
        grid=(m // tm,),
        in_specs=[rows(D_MODEL), rows(D_MODEL), rows(D_MODEL), rows(2 * D_MODEL),
                  const(w_rw), const(w_lru), const(w_mix), const(gate_b), const(g), const(b), const(wq)],
        out_specs=[rows(D_MODEL), rows(D_MODEL)],
        out_shape=[jax.ShapeDtypeStruct((m, D_MODEL), F32), jax.ShapeDtypeStruct((m, D_MODEL), q_dtype)],
        compiler_params=_params(("parallel",)),
        name="post",
    )(x, y_rw, y_lru, p_gate, w_rw, w_lru, w_mix, gate_b, g, b, wq)


def _attn_kernel(q_ref, k_ref, v_ref, o_ref):
    heads = [slice(h * XA_HEAD, (h + 1) * XA_HEAD) for h in range(XA_HEADS)]
    s = [_dot(q_ref[0, :, sl], k_ref[0, :, sl], _NT) * (XA_HEAD ** -0.5) for sl in heads]
    e = [jnp.exp(x - jnp.max(x, -1, keepdims=True)) for x in s]
    prob = [x / jnp.sum(x, -1, keepdims=True) for x in e]
    for sl, x in zip(heads, prob):
        o_ref[0, :, sl] = _dot(x, v_ref[0, :, sl]).astype(o_ref.dtype)


def _attn(q, k, v, tq):
    b, t, _ = q.shape
    return pl.pallas_call(
        _attn_kernel,
        grid=(b, t // tq),
        in_specs=[pl.BlockSpec((1, tq, D_MODEL), lambda i, j: (i, j, 0)),
                  pl.BlockSpec((1, N_MEM, D_MODEL), lambda i, j: (i, 0, 0)),
                  pl.BlockSpec((1, N_MEM, D_MODEL), lambda i, j: (i, 0, 0))],
        out_specs=pl.BlockSpec((1, tq, D_MODEL), lambda i, j: (i, j, 0)),
        out_shape=jax.ShapeDtypeStruct(q.shape, BF16),
        compiler_params=_params(("parallel", "parallel")),
        name="attn",
    )(q, k, v)


def _attn_decode_kernel(q_ref, k_ref, v_ref, o_ref):
    q = q_ref[...] * (XA_HEAD ** -0.5)
    s = jnp.sum(k_ref[...] * q[:, None, :, :], axis=-1, keepdims=True)
    e = jnp.exp(s - jnp.max(s, axis=1, keepdims=True))
    o_ref[...] = jnp.sum(e * v_ref[...], axis=1) / jnp.sum(e, axis=1)


def _attn_decode(q, k, v, l, nb):
    n = q.shape[0]
    kv = pl.BlockSpec((None, nb, N_MEM, XA_HEADS, XA_HEAD), lambda i: (l, i, 0, 0, 0))
    qo = pl.BlockSpec((nb, XA_HEADS, XA_HEAD), lambda i: (i, 0, 0))
    return pl.pallas_call(
        _attn_decode_kernel,
        grid=(n // nb,),
        in_specs=[qo, kv, kv],
        out_specs=qo,
        out_shape=jax.ShapeDtypeStruct(q.shape, F32),
        compiler_params=_params(("parallel",)),
        name="attn_decode",
    )(q, k, v)


def _tail_kernel(a_ref, x_ref, wo_ref, g2_ref, b2_ref, up_ref, down_ref, g3_ref, b3_ref, o_ref):
    x2 = _layer_norm(ALPHA * x_ref[...] + _dot(a_ref[...], wo_ref[...]), g2_ref[...], b2_ref[...])
    hdn = jnp.square(jnp.maximum(_dot(x2, up_ref[...]), 0.0))
    o_ref[...] = _layer_norm(ALPHA * x2 + _dot(hdn, down_ref[...]), g3_ref[...], b3_ref[...])


def _tail(a, x, l, wo, g2, b2, up, down, g3, b3, tm):
    m = a.shape[0]
    rows = pl.BlockSpec((tm, D_MODEL), lambda i: (i, 0))
    const = lambda z: _layer_spec(z, l)
    return pl.pallas_call(
        _tail_kernel,
        grid=(m // tm,),
        in_specs=[rows, rows, const(wo), const(g2), const(b2), const(up), const(down), const(g3), const(b3)],
        out_specs=rows,
        out_shape=jax.ShapeDtypeStruct((m, D_MODEL), F32),
        compiler_params=_params(("parallel",)),
        name="tail",
    )(a, x, wo, g2, b2, up, down, g3, b3)


def _row(x):
    return x.reshape(1, -1)


def _pair_blocks(wb):
    z = jnp.zeros((N_PAIR, RW_HEAD, RW_HEAD), wb.dtype)
    even, odd = wb[0::2], wb[1::2]
    top = jnp.concatenate([even, z], axis=2)
    bot = jnp.concatenate([z, odd], axis=2)
    return jnp.concatenate([top, bot], axis=1).astype(BF16)


def _layer_weights(l, W):
    z64 = jnp.zeros((64, D_MODEL), F32)
    rw = [_row(W['mu_shift'][l]), _row(W['rw_w0'][l]),
          jnp.concatenate([W['rw_w2'][l], z64], 0).astype(BF16), _row(W['rw_a0'][l]),
          jnp.concatenate([z64, W['rw_a2'][l]], 0).astype(BF16), W['rw_g2'][l].astype(BF16),
          _row(W['rw_k_k'][l]), _row(W['rw_k_a'][l]), _row(W['rw_r_k'][l]),
          _row(W['rw_lnx_g'][l]), _row(W['rw_lnx_b'][l])]
    lru = [W['lru_conv_w'][l], _row(W['lru_conv_b'][l]), _pair_blocks(W['lru_wa'][l]), _row(W['lru_ba'][l]),
           _pair_blocks(W['lru_wx'][l]), _row(W['lru_bx'][l]), _row(W['lru_lambda'][l])]
    return rw + lru


def _dense_weights(W):
    vec = lambda a: a.reshape(DEPTH, 1, -1)
    w_in = W['w_in']
    return dict(
        w_in=w_in.astype(BF16),
        rw_proj=W['rw_proj'].astype(BF16), lru_proj=W['lru_proj'].astype(BF16), w_mix=W['w_out_mix'].astype(BF16),
        gate_b=vec(W['mix_gate_b']), wq=W['xa_wq'].astype(BF16), wo=W['xa_wo'].astype(BF16),
        up=W['mlp_up'].astype(BF16), down=W['mlp_down'].astype(BF16),
        ln1_g=vec(W['ln1_g']), ln1_b=vec(W['ln1_b']), ln2_g=vec(W['ln2_g']), ln2_b=vec(W['ln2_b']),
        ln3_g=vec(W['ln3_g']), ln3_b=vec(W['ln3_b']))


def _trunk(x, mem_k, mem_v, states, W, P, tm, tq):
    batch, seq, _ = x.shape
    m = batch * seq
    x2 = x.reshape(m, D_MODEL)
    o_rw_st, o_shift_st, o_h_st, o_conv_st = [], [], [], []
    s_new_prev = None
    if states is not None:
        states = (jnp.transpose(states[0], (0, 2, 3, 4, 1)),) + tuple(states[1:])
    for l in range(DEPTH):
        mixw = _layer_weights(l, W)
        p_rw, p_uy, p_gate = _proj3(x2, l, P['w_in'], tm)
        if states is None:
            y_rw, y_lru, s_new, h_new = _mixer_prompt(p_rw, p_uy, mixw, batch, seq)
            h_new = h_new.reshape(batch, D_MODEL)
            conv_new = p_uy.reshape(batch, seq, C_UY)[:, seq - (CONV_W - 1):, 0:D_MODEL]
        else:
            s_rw, s_shift, s_h, s_conv = states
            rw7, y_lru, h_new = _mixer_sample_prep(
                p_rw, s_shift[l], p_uy, s_conv[l].reshape(batch, (CONV_W - 1) * D_MODEL), s_h[l], mixw)
            hd = (RW_HEADS, RW_HEAD, 1)
            rw7 = jnp.transpose(rw7, (0, 2, 1)).reshape(7, RW_HEADS, RW_HEAD, batch)
            s_new, y_rw = _rwkv_step(s_rw, l, rw7, W['rw_r_k'][l].reshape(hd), W['rw_lnx_g'][l].reshape(hd),
                                     W['rw_lnx_b'][l].reshape(hd), s_new_prev)
            s_new_prev = s_new
            y_rw = y_rw.reshape(D_MODEL, batch).T
            conv_new = jnp.concatenate([s_conv[l][:, 1:], p_uy[:, None, 0:D_MODEL]], axis=1)
        x2, q = _post(x2, y_rw, y_lru, p_gate, l, P['rw_proj'], P['lru_proj'], P['w_mix'], P['gate_b'],
                      P['ln1_g'], P['ln1_b'], P['wq'], min(tm, 512), BF16 if states is None else F32)
        if states is None:
            o = _attn(q.reshape(batch, seq, D_MODEL), mem_k[l], mem_v[l], tq)
        else:
            o = _attn_decode(q.reshape(batch, XA_HEADS, XA_HEAD), mem_k, mem_v, l, 4)
        x2 = _tail(o.reshape(m, D_MODEL), x2, l, P['wo'], P['ln2_g'], P['ln2_b'], P['up'], P['down'],
                   P['ln3_g'], P['ln3_b'], tm)
        o_rw_st.append(s_new)
        o_shift_st.append(p_rw.reshape(batch, seq, N_SHIFT)[:, seq - 1])
        o_h_st.append(h_new)
        o_conv_st.append(conv_new)
    rw_state = jnp.stack(o_rw_st, 0) if states is None else jnp.transpose(s_new_prev, (0, 4, 1, 2, 3))
    return x2.reshape(batch, seq, D_MODEL), (rw_state, jnp.stack(o_shift_st, 0),
                                             jnp.stack(o_h_st, 0), jnp.stack(o_conv_st, 0))


def kernel(x_prompt, x_sample, mem_prompt, cache_mem_k, cache_mem_v, state_rwkv, state_rwkv_shift, state_lru_h, state_lru_conv, w_in, mu_shift, rw_w0, rw_w2, rw_a0, rw_a2, rw_g2, rw_k_k, rw_k_a, rw_r_k, rw_lnx_g, rw_lnx_b, rw_proj, lru_conv_w, lru_conv_b, lru_wa, lru_ba, lru_wx, lru_bx, lru_lambda, lru_proj, mix_gate_b, w_out_mix, ln1_g, ln1_b, xa_wq, xa_wk, xa_wv, xa_wo, ln2_g, ln2_b, mlp_up, mlp_down, ln3_g, ln3_b):
    W = dict(w_in=w_in, mu_shift=mu_shift, rw_w0=rw_w0, rw_w2=rw_w2, rw_a0=rw_a0, rw_a2=rw_a2,
             rw_g2=rw_g2, rw_k_k=rw_k_k, rw_k_a=rw_k_a, rw_r_k=rw_r_k, rw_lnx_g=rw_lnx_g,
             rw_lnx_b=rw_lnx_b, rw_proj=rw_proj, lru_conv_w=lru_conv_w, lru_conv_b=lru_conv_b,
             lru_wa=lru_wa, lru_ba=lru_ba, lru_wx=lru_wx, lru_bx=lru_bx, lru_lambda=lru_lambda,
             lru_proj=lru_proj, mix_gate_b=mix_gate_b, w_out_mix=w_out_mix, ln1_g=ln1_g, ln1_b=ln1_b,
             xa_wq=xa_wq, xa_wo=xa_wo, ln2_g=ln2_g, ln2_b=ln2_b, mlp_up=mlp_up, mlp_down=mlp_down,
             ln3_g=ln3_g, ln3_b=ln3_b)
    bp, seq, _ = x_prompt.shape
    bs = x_sample.shape[0]
    mem2 = mem_prompt.reshape(bp * N_MEM, D_MODEL)
    tmem = min(512, bp * N_MEM)
    wk, wv = xa_wk.astype(BF16), xa_wv.astype(BF16)
    mem_k_p = jnp.stack([_mm(mem2, wk, l, tmem, 1024, "mem_k") for l in range(DEPTH)], 0)
    mem_v_p = jnp.stack([_mm(mem2, wv, l, tmem, 1024, "mem_v") for l in range(DEPTH)], 0)
    mem_k_p = mem_k_p.reshape(DEPTH, bp, N_MEM, D_MODEL)
    mem_v_p = mem_v_p.reshape(DEPTH, bp, N_MEM, D_MODEL)
    P = _dense_weights(W)
    tm_p = min(512, bp * seq)
    y_prompt, (p_rw, p_shift, p_h, p_conv) = _trunk(x_prompt, mem_k_p, mem_v_p, None, W, P, tm_p, min(512, seq))
    y_sample, (s_rw, s_shift, s_h, s_conv) = _trunk(
        x_sample, cache_mem_k, cache_mem_v, (state_rwkv, state_rwkv_shift, state_lru_h, state_lru_conv),
        W, P, bs, 1)
    kv_shape = (DEPTH, bp, N_MEM, XA_HEADS, XA_HEAD)
    return (y_prompt, y_sample, p_rw, p_shift, p_h, p_conv, mem_k_p.reshape(kv_shape), mem_v_p.reshape(kv_shape),
            s_rw, s_shift, s_h, s_conv)
```

```python
import functools

import jax
import jax.numpy as jnp
from jax import lax
from jax.experimental import pallas as pl
from jax.experimental.pallas import tpu as pltpu

F32 = jnp.float32
BF16 = jnp.bfloat16

D_MODEL = 1024
DEPTH = 2
RW_HEAD = 64
RW_HEADS = 16
N_PAIR = 8
LANES = 128
GN_EPS = 64e-5
LRU_C = 8.0
CONV_W = 4
N_MEM = 256
XA_HEADS = 4
XA_HEAD = 256
ALPHA = (2 * DEPTH) ** 0.25
LN_EPS = 1e-5
N_SHIFT = 3328
C_UY = 2048
CHUNK = 64
SUB = 16
DECAY_SCALE = 0.6065306597126334
VMEM_LIMIT = 56 * 1024 * 1024

_NT = (((1,), (1,)), ((), ()))
_TN = (((0,), (0,)), ((), ()))


def _params(sem):
    return pltpu.CompilerParams(dimension_semantics=sem, vmem_limit_bytes=VMEM_LIMIT)


def _sigmoid(x):
    return 1.0 / (1.0 + jnp.exp(-x))


def _softplus(x):
    return jnp.maximum(x, 0.0) + jnp.log(1.0 + jnp.exp(-jnp.abs(x)))


def _gelu_tanh(x):
    return 0.5 * x * (1.0 + jnp.tanh(0.7978845608028654 * (x + 0.044715 * (x * x * x))))


def _layer_norm(z, g, b):
    mu = jnp.mean(z, -1, keepdims=True)
    zc = z - mu
    var = jnp.mean(zc * zc, -1, keepdims=True)
    return zc * lax.rsqrt(var + LN_EPS) * g + b


def _dot(a, b, dims=None):
    a = a.astype(BF16)
    b = b.astype(BF16)
    if dims is None:
        return jnp.dot(a, b, preferred_element_type=F32)
    return lax.dot_general(a, b, dims, preferred_element_type=F32)


def _pair_ones():
    r = lax.broadcasted_iota(jnp.int32, (LANES, LANES), 0) // RW_HEAD
    c = lax.broadcasted_iota(jnp.int32, (LANES, LANES), 1) // RW_HEAD
    return jnp.where(r == c, 1.0, 0.0).astype(BF16)


def _segsum(x, ones):
    cols = [_dot(x[:, p * LANES:(p + 1) * LANES], ones) for p in range(x.shape[1] // LANES)]
    return jnp.concatenate(cols, axis=1) if len(cols) > 1 else cols[0]


def _rwkv_elem(ps, w, ones):
    r = ps[:, 0:1024]
    k = ps[:, 1024:2048]
    v = ps[:, 2048:3072]
    slab = ps[:, 3072:3200]
    xg = ps[:, 3200:3328]
    ld = -DECAY_SCALE * _sigmoid(w['w0'][...] + _dot(jnp.tanh(slab), w['w2'][...]))
    a_sig = _sigmoid(w['a0'][...] + _dot(slab, w['a2'][...]))
    g = _dot(_sigmoid(xg), w['g2'][...])
    kk = k * w['k_k'][...]
    kk = kk * lax.rsqrt(jnp.maximum(_segsum(kk * kk, ones), 1e-24))
    k_mod = k * (1.0 + (a_sig - 1.0) * w['k_a'][...])
    return r, k_mod, v, ld, -kk, kk * a_sig, g


def _lru_gates(xc, w, hp):
    sl = slice(hp * LANES, (hp + 1) * LANES)
    gr = _sigmoid(_dot(xc, w['wa'][hp]) + w['ba'][:, sl])
    gi = _sigmoid(_dot(xc, w['wx'][hp]) + w['bx'][:, sl])
    log_a = -LRU_C * _softplus(-w['lam'][:, sl]) * gr
    return jnp.exp(log_a), jnp.sqrt(1.0 - jnp.exp(2.0 * log_a)) * (gi * xc)


_RW_NAMES = ('mu', 'w0', 'w2', 'a0', 'a2', 'g2', 'k_k', 'k_a', 'r_k', 'ln_g', 'ln_b')
_LRU_NAMES = ('cw', 'cb', 'wa', 'ba', 'wx', 'bx', 'lam')


def _stack(x, m0):
    return jnp.concatenate([jnp.where(m0, x, 0.0), jnp.where(m0, 0.0, x)], axis=0)


def _mixer_prompt_kernel(*refs):
    n_w = len(_RW_NAMES) + len(_LRU_NAMES)
    prw_ref, puy_ref = refs[0], refs[1]
    w = dict(zip(_RW_NAMES + _LRU_NAMES, refs[2:2 + n_w]))
    yrw_ref, ylru_ref, sout_ref, hout_ref = refs[2 + n_w:6 + n_w]
    pbuf, ubuf, s_scr, h_scr = refs[6 + n_w:]
    C = CHUNK
    c = pl.program_id(1)

    @pl.when(c == 0)
    def _():
        pbuf[0:8, :] = jnp.zeros((8, N_SHIFT), F32)
        ubuf[0:8, :] = jnp.zeros((8, D_MODEL), F32)
        s_scr[...] = jnp.zeros_like(s_scr)
        h_scr[...] = jnp.zeros_like(h_scr)

    ones = _pair_ones()
    row = lax.broadcasted_iota(jnp.int32, (C, 1), 0)

    lru_groups = list(range(N_PAIR))

    def lru_next():
        if not lru_groups:
            return
        hp = lru_groups.pop(0)
        sl = slice(hp * LANES, (hp + 1) * LANES)
        u_in = puy_ref[:, sl]
        ubuf[8:8 + C, sl] = u_in
        cw = w['cw']
        xc = (w['cb'][:, sl] + ubuf[5:5 + C, sl] * cw[0:1, sl] + ubuf[6:6 + C, sl] * cw[1:2, sl]
              + ubuf[7:7 + C, sl] * cw[2:3, sl] + u_in * cw[3:4, sl])
        ubuf[0:8, sl] = u_in[C - 8:C, :]
        a_l, b_l = _lru_gates(xc, w, hp)
        d = 1
        while d < C:
            keep = row >= d
            a_sh = jnp.where(keep, pltpu.roll(a_l, d, 0), 1.0)
            b_sh = jnp.where(keep, pltpu.roll(b_l, d, 0), 0.0)
            b_l = a_l * b_sh + b_l
            a_l = a_l * a_sh
            d *= 2
        h = a_l * h_scr[0:1, sl] + b_l
        h_scr[0:1, sl] = h[C - 1:C, :]
        y_lru = h * _gelu_tanh(puy_ref[:, D_MODEL + hp * LANES:D_MODEL + (hp + 1) * LANES])
        ylru_ref[:, sl] = y_lru.astype(ylru_ref.dtype)

    p = prw_ref[...]
    pbuf[8:8 + C, :] = p
    prev = pbuf[7:7 + C, :]
    ps = p + w['mu'][...] * (prev - p)
    r, k_mod, v, ld, a_vec, b_vec, g = _rwkv_elem(ps, w, ones)
    pbuf[0:8, :] = p[C - 8:C, :]

    tri = jnp.where(lax.broadcasted_iota(jnp.int32, (C, C), 0) >= lax.broadcasted_iota(jnp.int32, (C, C), 1),
                    1.0, 0.0).astype(BF16)
    l_hi = ld.astype(BF16)
    l_r1 = ld - l_hi.astype(F32)
    cum = _dot(tri, l_hi) + _dot(tri, l_r1.astype(BF16))
    cum_last = cum[C - 1:C, :]
    e_neg = jnp.exp(-cum)
    at = a_vec * jnp.exp(cum - ld)
    rt = r * jnp.exp(cum)
    bt = b_vec * e_neg
    kt = k_mod * e_neg
    wc = jnp.exp(cum_last)
    bh = bt * wc
    kh = kt * wc
    bonus_dot = _segsum(r * k_mod * w['r_k'][...], ones)

    n = 2 * C
    ri = lax.broadcasted_iota(jnp.int32, (n, n), 0)
    ci = lax.broadcasted_iota(jnp.int32, (n, n), 1)
    low_strict = ri > ci
    low_incl = ri >= ci
    diag_blk = (ri // SUB) == (ci // SUB)
    eye = jnp.where(ri == ci, 1.0, 0.0)
    m0 = lax.broadcasted_iota(jnp.int32, (C, LANES), 1) < RW_HEAD

    pairs = range(N_PAIR)
    sls = [slice(hp * LANES, (hp + 1) * LANES) for hp in pairs]

    def wide(xs, ys_list):
        outs = [[None] * N_PAIR for _ in ys_list]
        for p in pairs:
            o = _dot(xs[p], jnp.concatenate([ys[p] for ys in ys_list], axis=1))
            for i in range(len(ys_list)):
                outs[i][p] = o[:, i * n:(i + 1) * n]
        return outs

    vs_ = [_stack(v[:, sl], m0) for sl in sls]
    s_bd = [s_scr[hp] for hp in pairs]
    gram = [_dot(jnp.concatenate([at[:, sl], rt[:, sl]], axis=0),
                 jnp.concatenate([_stack(bt[:, sl], m0), _stack(kt[:, sl], m0), s_bd[hp]], axis=0), _NT)
            for hp, sl in zip(pairs, sls)]
    lru_next()
    n_ab = [jnp.where(low_strict, _stack(gm[0:C, 0:n], m0), 0.0) for gm in gram]
    a_ak = [jnp.where(low_strict, _stack(gm[0:C, n:2 * n], m0), 0.0) for gm in gram]
    a_s = [_stack(gm[0:C, 2 * n:3 * n], m0) for gm in gram]
    r_bk = [jnp.concatenate([jnp.where(low_incl, _stack(gm[C:n, 0:n], m0), 0.0),
                             jnp.where(low_incl, _stack(gm[C:n, n:2 * n], m0), 0.0)], axis=1) for gm in gram]
    r_s = [_stack(gm[C:n, 2 * n:3 * n], m0) for gm in gram]
    rhs_u = [x + y for x, y in zip(a_s, wide(a_ak, [vs_])[0])]
    lru_next()

    n_d = [jnp.where(diag_blk, x, 0.0) for x in n_ab]
    n_o = [x - d for x, d in zip(n_ab, n_d)]
    d_inv = [eye + d for d in n_d]
    q = wide(n_d, [n_d])[0]
    lru_next()
    for _ in range(2):
        q, qd = wide(q, [q, d_inv])
        d_inv = [d + x for d, x in zip(d_inv, qd)]
        lru_next()
    d_inv = [d + x for d, x in zip(d_inv, wide(q, [d_inv])[0])]
    lru_next()
    m1, z = wide(d_inv, [n_o, rhs_u])
    lru_next()
    m2, mz = wide(m1, [m1, z])
    lru_next()
    z = [x + y for x, y in zip(z, mz)]
    u = [x + y for x, y in zip(z, wide(m2, [z])[0])]
    uv = [jnp.concatenate([x, vv], axis=0) for x, vv in zip(u, vs_)]
    y_s = [rs + _dot(rb, x) for rs, rb, x in zip(r_s, r_bk, uv)]
    lru_next()
    s_new = [s_bd[hp] * wc[:, sl]
             + _dot(uv[hp], jnp.concatenate([_stack(bh[:, sl], m0), _stack(kh[:, sl], m0)], axis=0), _TN)
             for hp, sl in zip(pairs, sls)]
    for hp in pairs:
        s_scr[hp] = s_new[hp]
    while lru_groups:
        lru_next()

    y = [x[0:C] + x[C:n] for x in y_s]
    mu = [_dot(x, ones) * (1.0 / RW_HEAD) for x in y]
    yc = [x - m for x, m in zip(y, mu)]
    var = [_dot(x * x, ones) * (1.0 / RW_HEAD) for x in yc]
    for hp, sl in zip(pairs, sls):
        yn = yc[hp] * lax.rsqrt(var[hp] + GN_EPS) * w['ln_g'][:, sl] + w['ln_b'][:, sl]
        yrw_ref[:, sl] = ((yn + bonus_dot[:, sl] * v[:, sl]) * g[:, sl]).astype(yrw_ref.dtype)

    @pl.when(c == pl.num_programs(1) - 1)
    def _():
        hout_ref[0] = h_scr[0:1, :]
        for hp in range(N_PAIR):
            s_bd = s_scr[hp]
            sout_ref[0, 2 * hp] = s_bd[0:RW_HEAD, 0:RW_HEAD]
            sout_ref[0, 2 * hp + 1] = s_bd[RW_HEAD:LANES, RW_HEAD:LANES]


def _mixer_prompt(p_rw, p_uy, weights, batch, seq):
    C = CHUNK
    nc = seq // C
    row_map = lambda b, c: (b * nc + c, 0)
    w_specs = [pl.BlockSpec(x.shape, (lambda b, c, nd=x.ndim: (0,) * nd)) for x in weights]
    return pl.pallas_call(
        _mixer_prompt_kernel,
        grid=(batch, nc),
        in_specs=[pl.BlockSpec((C, N_SHIFT), row_map), pl.BlockSpec((C, C_UY), row_map)] + w_specs,
        out_specs=[pl.BlockSpec((C, D_MODEL), row_map), pl.BlockSpec((C, D_MODEL), row_map),
                   pl.BlockSpec((1, RW_HEADS, RW_HEAD, RW_HEAD), lambda b, c: (b, 0, 0, 0)),
                   pl.BlockSpec((1, 1, D_MODEL), lambda b, c: (b, 0, 0))],
        out_shape=[jax.ShapeDtypeStruct((batch * seq, D_MODEL), BF16),
                   jax.ShapeDtypeStruct((batch * seq, D_MODEL), BF16),
                   jax.ShapeDtypeStruct((batch, RW_HEADS, RW_HEAD, RW_HEAD), F32),
                   jax.ShapeDtypeStruct((batch, 1, D_MODEL), F32)],
        scratch_shapes=[pltpu.VMEM((8 + C, N_SHIFT), F32), pltpu.VMEM((8 + C, D_MODEL), F32),
                        pltpu.VMEM((N_PAIR, LANES, LANES), F32), pltpu.VMEM((8, D_MODEL), F32)],
        compiler_params=_params(("parallel", "arbitrary")),
        name="mixer_prompt",
    )(p_rw, p_uy, *weights)


def _mixer_sample_prep_kernel(*refs):
    n_w = len(_RW_NAMES) + len(_LRU_NAMES)
    prw_ref, shift_ref, puy_ref, conv_ref, h0_ref = refs[0:5]
    w = dict(zip(_RW_NAMES + _LRU_NAMES, refs[5:5 + n_w]))
    rw_ref, ylru_ref, hnew_ref = refs[5 + n_w:]
    ones = _pair_ones()
    p = prw_ref[...]
    ps = p + w['mu'][...] * (shift_ref[...] - p)
    r, k_mod, v, ld, a_vec, b_vec, g = _rwkv_elem(ps, w, ones)
    for i, x in enumerate((r, k_mod, v, jnp.exp(ld), a_vec, b_vec, g)):
        rw_ref[i] = x
    cw = w['cw']
    for hp in range(N_PAIR):
        sl = slice(hp * LANES, (hp + 1) * LANES)
        prev = [conv_ref[:, j * D_MODEL + hp * LANES:j * D_MODEL + (hp + 1) * LANES] for j in range(CONV_W - 1)]
        xc = (w['cb'][:, sl] + prev[0] * cw[0:1, sl] + prev[1] * cw[1:2, sl] + prev[2] * cw[2:3, sl]
              + puy_ref[:, sl] * cw[3:4, sl])
        a_l, b_l = _lru_gates(xc, w, hp)
        h = a_l * h0_ref[:, sl] + b_l
        hnew_ref[:, sl] = h
        ylru_ref[:, sl] = h * _gelu_tanh(puy_ref[:, D_MODEL + hp * LANES:D_MODEL + (hp + 1) * LANES])


def _mixer_sample_prep(p_rw, shift, p_uy, conv, h0, weights):
    nb = p_rw.shape[0]
    ins = [p_rw, shift, p_uy, conv, h0] + list(weights)
    full = lambda x: pl.BlockSpec(x.shape, (lambda i, nd=x.ndim: (0,) * nd))
    return pl.pallas_call(
        _mixer_sample_prep_kernel,
        grid=(1,),
        in_specs=[full(x) for x in ins],
        out_specs=[pl.BlockSpec((7, nb, D_MODEL), lambda i: (0, 0, 0)),
                   pl.BlockSpec((nb, D_MODEL), lambda i: (0, 0)),
                   pl.BlockSpec((nb, D_MODEL), lambda i: (0, 0))],
        out_shape=[jax.ShapeDtypeStruct((7, nb, D_MODEL), F32),
                   jax.ShapeDtypeStruct((nb, D_MODEL), F32),
                   jax.ShapeDtypeStruct((nb, D_MODEL), F32)],
        compiler_params=_params(("arbitrary",)),
        name="mixer_sample_prep",
    )(*ins)


def _rwkv_step_kernel(*refs, n_carry):
    if n_carry:
        s_ref, x_ref, rk_ref, lng_ref, lnb_ref, carry_ref, sout_ref, y_ref = refs
        sout_ref[0:n_carry] = carry_ref[...]
    else:
        s_ref, x_ref, rk_ref, lng_ref, lnb_ref, sout_ref, y_ref = refs
    s = s_ref[0]
    r, k, v, wd, a, b, g = (x_ref[i, 0] for i in range(7))
    sa = jnp.sum(s * a[None, :, :], axis=1)
    s_new = s * wd[None, :, :] + sa[:, None, :] * b[None, :, :] + v[:, None, :] * k[None, :, :]
    sout_ref[n_carry, 0] = s_new
    y = jnp.sum(s_new * r[None, :, :], axis=1)
    mu = jnp.mean(y, axis=0, keepdims=True)
    yc = y - mu
    var = jnp.mean(yc * yc, axis=0, keepdims=True)
    yn = yc * lax.rsqrt(var + GN_EPS) * lng_ref[0] + lnb_ref[0]
    bonus = jnp.sum(r * k * rk_ref[0], axis=0, keepdims=True) * v
    y_ref[0] = (yn + bonus) * g


def _rwkv_step(state, l, rw7, r_k, ln_g, ln_b, carry):
    n = state.shape[-1]
    blk = (RW_HEAD, RW_HEAD, n)
    head = lambda *lead: pl.BlockSpec(lead + (RW_HEAD, 1), lambda i: (i, 0, 0))
    in_specs = [pl.BlockSpec((None, 1) + blk, lambda i: (l, i, 0, 0, 0)),
                pl.BlockSpec((7, 1, RW_HEAD, n), lambda i: (0, i, 0, 0)),
                head(1), head(1), head(1)]
    args = [state, rw7, r_k, ln_g, ln_b]
    if carry is not None:
        in_specs.append(pl.BlockSpec((l, 1) + blk, lambda i: (0, i, 0, 0, 0)))
        args.append(carry)
    return pl.pallas_call(
        functools.partial(_rwkv_step_kernel, n_carry=0 if carry is None else l),
        grid=(RW_HEADS,),
        in_specs=in_specs,
        out_specs=[pl.BlockSpec((l + 1, 1) + blk, lambda i: (0, i, 0, 0, 0)),
                   pl.BlockSpec((1, RW_HEAD, n), lambda i: (i, 0, 0))],
        out_shape=[jax.ShapeDtypeStruct((l + 1, RW_HEADS) + blk, F32),
                   jax.ShapeDtypeStruct((RW_HEADS, RW_HEAD, n), F32)],
        compiler_params=_params(("parallel",)),
        name="rwkv_step",
    )(*args)


def _mm_kernel(x_ref, w_ref, o_ref):
    o_ref[...] = _dot(x_ref[...], w_ref[...])


def _mm(x, w, l, tm, tn, name):
    m, kd = x.shape
    n = w.shape[2]
    return pl.pallas_call(
        _mm_kernel,
        grid=(m // tm, n // tn),
        in_specs=[pl.BlockSpec((tm, kd), lambda i, j: (i, 0)),
                  pl.BlockSpec((None, kd, tn), lambda i, j: (l, 0, j))],
        out_specs=pl.BlockSpec((tm, tn), lambda i, j: (i, j)),
        out_shape=jax.ShapeDtypeStruct((m, n), F32),
        compiler_params=_params(("parallel", "parallel")),
        name=name,
    )(x, w)


_PROJ_COLS = (0, N_SHIFT, N_SHIFT + C_UY, N_SHIFT + C_UY + 2 * D_MODEL)


def _proj3_kernel(x_ref, w_ref, prw_ref, puy_ref, pg_ref):
    x = x_ref[...].astype(BF16)
    for o_ref, c0, c1 in zip((prw_ref, puy_ref, pg_ref), _PROJ_COLS[:-1], _PROJ_COLS[1:]):
        o_ref[...] = _dot(x, w_ref[:, c0:c1])


def _proj3(x, l, w_in, tm):
    m = x.shape[0]
    rows = lambda width: pl.BlockSpec((tm, width), lambda i: (i, 0))
    widths = [c1 - c0 for c0, c1 in zip(_PROJ_COLS[:-1], _PROJ_COLS[1:])]
    return pl.pallas_call(
        _proj3_kernel,
        grid=(m // tm,),
        in_specs=[rows(D_MODEL), _layer_spec(w_in, l)],
        out_specs=[rows(n) for n in widths],
        out_shape=[jax.ShapeDtypeStruct((m, n), F32) for n in widths],
        compiler_params=_params(("parallel",)),
        name="proj_in",
    )(x, w_in)


def _post_kernel(x_ref, yrw_ref, ylru_ref, pg_ref, wrw_ref, wlru_ref, wmix_ref, gb_ref, g_ref, b_ref, wq_ref,
                 o_ref, q_ref):
    o_rw = _dot(yrw_ref[...], wrw_ref[...])
    o_lru = _dot(ylru_ref[...], wlru_ref[...])
    gates = _sigmoid(pg_ref[...] + gb_ref[...])
    mix = _dot(gates[:, 0:D_MODEL] * o_rw + gates[:, D_MODEL:2 * D_MODEL] * o_lru, wmix_ref[...])
    x1 = _layer_norm(ALPHA * x_ref[...] + mix, g_ref[...], b_ref[...])
    o_ref[...] = x1
    q_ref[...] = _dot(x1, wq_ref[...]).astype(q_ref.dtype)


def _layer_spec(a, l):
    return pl.BlockSpec((None,) + a.shape[1:], lambda *_: (l, 0, 0))


def _post(x, y_rw, y_lru, p_gate, l, w_rw, w_lru, w_mix, gate_b, g, b, wq, tm, q_dtype):
    m = x.shape[0]
    rows = lambda width: pl.BlockSpec((tm, width), lambda i: (i, 0))
    const = lambda a: _layer_spec(a, l)
    return pl.pallas_call(
        _post_kernel,
        grid=(m // tm,),
        in_specs=[rows(D_MODEL), rows(D_MODEL), rows(D_MODEL), rows(2 * D_MODEL),
                  const(w_rw), const(w_lru), const(w_mix), const(gate_b), const(g), const(b), const(wq)],
        out_specs=[rows(D_MODEL), rows(D_MODEL)],
        out_shape=[jax.ShapeDtypeStruct((m, D_MODEL), F32), jax.ShapeDtypeStruct((m, D_MODEL), q_dtype)],
        compiler_params=_params(("parallel",)),
        name="post",
    )(x, y_rw, y_lru, p_gate, w_rw, w_lru, w_mix, gate_b, g, b, wq)


def _attn_kernel(q_ref, k_ref, v_ref, o_ref):
    heads = [slice(h * XA_HEAD, (h + 1) * XA_HEAD) for h in range(XA_HEADS)]
    s = [_dot(q_ref[0, :, sl], k_ref[0, :, sl], _NT) * (XA_HEAD ** -0.5) for sl in heads]
    e = [jnp.exp(x - jnp.max(x, -1, keepdims=True)) for x in s]
    prob = [x / jnp.sum(x, -1, keepdims=True) for x in e]
    for sl, x in zip(heads, prob):
        o_ref[0, :, sl] = _dot(x, v_ref[0, :, sl]).astype(o_ref.dtype)


def _attn(q, k, v, tq):
    b, t, _ = q.shape
    return pl.pallas_call(
        _attn_kernel,
        grid=(b, t // tq),
        in_specs=[pl.BlockSpec((1, tq, D_MODEL), lambda i, j: (i, j, 0)),
                  pl.BlockSpec((1, N_MEM, D_MODEL), lambda i, j: (i, 0, 0)),
                  pl.BlockSpec((1, N_MEM, D_MODEL), lambda i, j: (i, 0, 0))],
        out_specs=pl.BlockSpec((1, tq, D_MODEL), lambda i, j: (i, j, 0)),
        out_shape=jax.ShapeDtypeStruct(q.shape, BF16),
        compiler_params=_params(("parallel", "parallel")),
        name="attn",
    )(q, k, v)


def _attn_decode_kernel(q_ref, k_ref, v_ref, o_ref):
    q = q_ref[...] * (XA_HEAD ** -0.5)
    s = jnp.sum(k_ref[...] * q[:, None, :, :], axis=-1, keepdims=True)
    e = jnp.exp(s - jnp.max(s, axis=1, keepdims=True))
    o_ref[...] = jnp.sum(e * v_ref[...], axis=1) / jnp.sum(e, axis=1)


def _attn_decode(q, k, v, l, nb):
    n = q.shape[0]
    kv = pl.BlockSpec((None, nb, N_MEM, XA_HEADS, XA_HEAD), lambda i: (l, i, 0, 0, 0))
    qo = pl.BlockSpec((nb, XA_HEADS, XA_HEAD), lambda i: (i, 0, 0))
    return pl.pallas_call(
        _attn_decode_kernel,
        grid=(n // nb,),
        in_specs=[qo, kv, kv],
        out_specs=qo,
        out_shape=jax.ShapeDtypeStruct(q.shape, F32),
        compiler_params=_params(("parallel",)),
        name="attn_decode",
    )(q, k, v)


def _tail_kernel(a_ref, x_ref, wo_ref, g2_ref, b2_ref, up_ref, down_ref, g3_ref, b3_ref, o_ref):
    x2 = _layer_norm(ALPHA * x_ref[...] + _dot(a_ref[...], wo_ref[...]), g2_ref[...], b2_ref[...])
    hdn = jnp.square(jnp.maximum(_dot(x2, up_ref[...]), 0.0))
    o_ref[...] = _layer_norm(ALPHA * x2 + _dot(hdn, down_ref[...]), g3_ref[...], b3_ref[...])


def _tail(a, x, l, wo, g2, b2, up, down, g3, b3, tm):
    m = a.shape[0]
    rows = pl.BlockSpec((tm, D_MODEL), lambda i: (i, 0))
    const = lambda z: _layer_spec(z, l)
    return pl.pallas_call(
        _tail_kernel,
        grid=(m // tm,),
        in_specs=[rows, rows, const(wo), const(g2), const(b2), const(up), const(down), const(g3), const(b3)],
        out_specs=rows,
        out_shape=jax.ShapeDtypeStruct((m, D_MODEL), F32),
        compiler_params=_params(("parallel",)),
        name="tail",
    )(a, x, wo, g2, b2, up, down, g3, b3)


def _row(x):
    return x.reshape(1, -1)


def _pair_blocks(wb):
    z = jnp.zeros((N_PAIR, RW_HEAD, RW_HEAD), wb.dtype)
    even, odd = wb[0::2], wb[1::2]
    top = jnp.concatenate([even, z], axis=2)
    bot = jnp.concatenate([z, odd], axis=2)
    return jnp.concatenate([top, bot], axis=1).astype(BF16)


def _layer_weights(l, W):
    z64 = jnp.zeros((64, D_MODEL), F32)
    rw = [_row(W['mu_shift'][l]), _row(W['rw_w0'][l]),
          jnp.concatenate([W['rw_w2'][l], z64], 0).astype(BF16), _row(W['rw_a0'][l]),
          jnp.concatenate([z64, W['rw_a2'][l]], 0).astype(BF16), W['rw_g2'][l].astype(BF16),
          _row(W['rw_k_k'][l]), _row(W['rw_k_a'][l]), _row(W['rw_r_k'][l]),
          _row(W['rw_lnx_g'][l]), _row(W['rw_lnx_b'][l])]
    lru = [W['lru_conv_w'][l], _row(W['lru_conv_b'][l]), _pair_blocks(W['lru_wa'][l]), _row(W['lru_ba'][l]),
           _pair_blocks(W['lru_wx'][l]), _row(W['lru_bx'][l]), _row(W['lru_lambda'][l])]
    return rw + lru


def _dense_weights(W):
    vec = lambda a: a.reshape(DEPTH, 1, -1)
    w_in = W['w_in']
    return dict(
        w_in=w_in.astype(BF16),
        rw_proj=W['rw_proj'].astype(BF16), lru_proj=W['lru_proj'].astype(BF16), w_mix=W['w_out_mix'].astype(BF16),
        gate_b=vec(W['mix_gate_b']), wq=W['xa_wq'].astype(BF16), wo=W['xa_wo'].astype(BF16),
        up=W['mlp_up'].astype(BF16), down=W['mlp_down'].astype(BF16),
        ln1_g=vec(W['ln1_g']), ln1_b=vec(W['ln1_b']), ln2_g=vec(W['ln2_g']), ln2_b=vec(W['ln2_b']),
        ln3_g=vec(W['ln3_g']), ln3_b=vec(W['ln3_b']))


def _trunk(x, mem_k, mem_v, states, W, P, tm, tq):
    batch, seq, _ = x.shape
    m = batch * seq
    x2 = x.reshape(m, D_MODEL)
    o_rw_st, o_shift_st, o_h_st, o_conv_st = [], [], [], []
    s_new_prev = None
    if states is not None:
        states = (jnp.transpose(states[0], (0, 2, 3, 4, 1)),) + tuple(states[1:])
    for l in range(DEPTH):
        mixw = _layer_weights(l, W)
        p_rw, p_uy, p_gate = _proj3(x2, l, P['w_in'], tm)
        if states is None:
            y_rw, y_lru, s_new, h_new = _mixer_prompt(p_rw, p_uy, mixw, batch, seq)
            h_new = h_new.reshape(batch, D_MODEL)
            conv_new = p_uy.reshape(batch, seq, C_UY)[:, seq - (CONV_W - 1):, 0:D_MODEL]
        else:
            s_rw, s_shift, s_h, s_conv = states
            rw7, y_lru, h_new = _mixer_sample_prep(
                p_rw, s_shift[l], p_uy, s_conv[l].reshape(batch, (CONV_W - 1) * D_MODEL), s_h[l], mixw)
            hd = (RW_HEADS, RW_HEAD, 1)
            rw7 = jnp.transpose(rw7, (0, 2, 1)).reshape(7, RW_HEADS, RW_HEAD, batch)
            s_new, y_rw = _rwkv_step(s_rw, l, rw7, W['rw_r_k'][l].reshape(hd), W['rw_lnx_g'][l].reshape(hd),
                                     W['rw_lnx_b'][l].reshape(hd), s_new_prev)
            s_new_prev = s_new
            y_rw = y_rw.reshape(D_MODEL, batch).T
            conv_new = jnp.concatenate([s_conv[l][:, 1:], p_uy[:, None, 0:D_MODEL]], axis=1)
        x2, q = _post(x2, y_rw, y_lru, p_gate, l, P['rw_proj'], P['lru_proj'], P['w_mix'], P['gate_b'],
                      P['ln1_g'], P['ln1_b'], P['wq'], min(tm, 512), BF16 if states is None else F32)
        if states is None:
            o = _attn(q.reshape(batch, seq, D_MODEL), mem_k[l], mem_v[l], tq)
        else:
            o = _attn_decode(q.reshape(batch, XA_HEADS, XA_HEAD), mem_k, mem_v, l, 4)
        x2 = _tail(o.reshape(m, D_MODEL), x2, l, P['wo'], P['ln2_g'], P['ln2_b'], P['up'], P['down'],
                   P['ln3_g'], P['ln3_b'], tm)
        o_rw_st.append(s_new)
        o_shift_st.append(p_rw.reshape(batch, seq, N_SHIFT)[:, seq - 1])
        o_h_st.append(h_new)
        o_conv_st.append(conv_new)
    rw_state = jnp.stack(o_rw_st, 0) if states is None else jnp.transpose(s_new_prev, (0, 4, 1, 2, 3))
    return x2.reshape(batch, seq, D_MODEL), (rw_state, jnp.stack(o_shift_st, 0),
                                             jnp.stack(o_h_st, 0), jnp.stack(o_conv_st, 0))


def kernel(x_prompt, x_sample, mem_prompt, cache_mem_k, cache_mem_v, state_rwkv, state_rwkv_shift, state_lru_h, state_lru_conv, w_in, mu_shift, rw_w0, rw_w2, rw_a0, rw_a2, rw_g2, rw_k_k, rw_k_a, rw_r_k, rw_lnx_g, rw_lnx_b, rw_proj, lru_conv_w, lru_conv_b, lru_wa, lru_ba, lru_wx, lru_bx, lru_lambda, lru_proj, mix_gate_b, w_out_mix, ln1_g, ln1_b, xa_wq, xa_wk, xa_wv, xa_wo, ln2_g, ln2_b, mlp_up, mlp_down, ln3_g, ln3_b):
    W = dict(w_in=w_in, mu_shift=mu_shift, rw_w0=rw_w0, rw_w2=rw_w2, rw_a0=rw_a0, rw_a2=rw_a2,
             rw_g2=rw_g2, rw_k_k=rw_k_k, rw_k_a=rw_k_a, rw_r_k=rw_r_k, rw_lnx_g=rw_lnx_g,
             rw_lnx_b=rw_lnx_b, rw_proj=rw_proj, lru_conv_w=lru_conv_w, lru_conv_b=lru_conv_b,
             lru_wa=lru_wa, lru_ba=lru_ba, lru_wx=lru_wx, lru_bx=lru_bx, lru_lambda=lru_lambda,
             lru_proj=lru_proj, mix_gate_b=mix_gate_b, w_out_mix=w_out_mix, ln1_g=ln1_g, ln1_b=ln1_b,
             xa_wq=xa_wq, xa_wo=xa_wo, ln2_g=ln2_g, ln2_b=ln2_b, mlp_up=mlp_up, mlp_down=mlp_down,
             ln3_g=ln3_g, ln3_b=ln3_b)
    bp, seq, _ = x_prompt.shape
    bs = x_sample.shape[0]
    mem2 = mem_prompt.reshape(bp * N_MEM, D_MODEL)
    tmem = min(512, bp * N_MEM)
    wk, wv = xa_wk.astype(BF16), xa_wv.astype(BF16)
    mem_k_p = jnp.stack([_mm(mem2, wk, l, tmem, 1024, "mem_k") for l in range(DEPTH)], 0)
    mem_v_p = jnp.stack([_mm(mem2, wv, l, tmem, 1024, "mem_v") for l in range(DEPTH)], 0)
    mem_k_p = mem_k_p.reshape(DEPTH, bp, N_MEM, D_MODEL)
    mem_v_p = mem_v_p.reshape(DEPTH, bp, N_MEM, D_MODEL)
    P = _dense_weights(W)
    tm_p = min(512, bp * seq)
    y_prompt, (p_rw, p_shift, p_h, p_conv) = _trunk(x_prompt, mem_k_p, mem_v_p, None, W, P, tm_p, min(512, seq))
    y_sample, (s_rw, s_shift, s_h, s_conv) = _trunk(
        x_sample, cache_mem_k, cache_mem_v, (state_rwkv, state_rwkv_shift, state_lru_h, state_lru_conv),
        W, P, bs, 1)
    kv_shape = (DEPTH, bp, N_MEM, XA_HEADS, XA_HEAD)
    return (y_prompt, y_sample, p_rw, p_shift, p_h, p_conv, mem_k_p.reshape(kv_shape), mem_v_p.reshape(kv_shape),
            s_rw, s_shift, s_h, s_conv)
```

```python
import functools

import jax
import jax.numpy as jnp
from jax import lax
from jax.experimental import pallas as pl
from jax.experimental.pallas import tpu as pltpu

F32 = jnp.float32
BF16 = jnp.bfloat16

D_MODEL = 1024
DEPTH = 2
RW_HEAD = 64
RW_HEADS = 16
N_PAIR = 8
LANES = 128
GN_EPS = 64e-5
LRU_C = 8.0
CONV_W = 4
N_MEM = 256
XA_HEADS = 4
XA_HEAD = 256
ALPHA = (2 * DEPTH) ** 0.25
LN_EPS = 1e-5
N_SHIFT = 3328
C_UY = 2048
CHUNK = 64
SUB = 16
DECAY_SCALE = 0.6065306597126334
VMEM_LIMIT = 56 * 1024 * 1024

_NT = (((1,), (1,)), ((), ()))
_TN = (((0,), (0,)), ((), ()))


def _params(sem):
    return pltpu.CompilerParams(dimension_semantics=sem, vmem_limit_bytes=VMEM_LIMIT)


def _sigmoid(x):
    return 1.0 / (1.0 + jnp.exp(-x))


def _softplus(x):
    return jnp.maximum(x, 0.0) + jnp.log(1.0 + jnp.exp(-jnp.abs(x)))


def _gelu_tanh(x):
    return 0.5 * x * (1.0 + jnp.tanh(0.7978845608028654 * (x + 0.044715 * (x * x * x))))


def _layer_norm(z, g, b):
    mu = jnp.mean(z, -1, keepdims=True)
    zc = z - mu
    var = jnp.mean(zc * zc, -1, keepdims=True)
    return zc * lax.rsqrt(var + LN_EPS) * g + b


def _dot(a, b, dims=None):
    a = a.astype(BF16)
    b = b.astype(BF16)
    if dims is None:
        return jnp.dot(a, b, preferred_element_type=F32)
    return lax.dot_general(a, b, dims, preferred_element_type=F32)


def _pair_ones():
    r = lax.broadcasted_iota(jnp.int32, (LANES, LANES), 0) // RW_HEAD
    c = lax.broadcasted_iota(jnp.int32, (LANES, LANES), 1) // RW_HEAD
    return jnp.where(r == c, 1.0, 0.0).astype(BF16)


def _segsum(x, ones):
    cols = [_dot(x[:, p * LANES:(p + 1) * LANES], ones) for p in range(x.shape[1] // LANES)]
    return jnp.concatenate(cols, axis=1) if len(cols) > 1 else cols[0]


def _rwkv_elem(ps, w, ones):
    r = ps[:, 0:1024]
    k = ps[:, 1024:2048]
    v = ps[:, 2048:3072]
    slab = ps[:, 3072:3200]
    xg = ps[:, 3200:3328]
    ld = -DECAY_SCALE * _sigmoid(w['w0'][...] + _dot(jnp.tanh(slab), w['w2'][...]))
    a_sig = _sigmoid(w['a0'][...] + _dot(slab, w['a2'][...]))
    g = _dot(_sigmoid(xg), w['g2'][...])
    kk = k * w['k_k'][...]
    kk = kk * lax.rsqrt(jnp.maximum(_segsum(kk * kk, ones), 1e-24))
    k_mod = k * (1.0 + (a_sig - 1.0) * w['k_a'][...])
    return r, k_mod, v, ld, -kk, kk * a_sig, g


def _lru_gates(xc, w, hp):
    sl = slice(hp * LANES, (hp + 1) * LANES)
    gr = _sigmoid(_dot(xc, w['wa'][hp]) + w['ba'][:, sl])
    gi = _sigmoid(_dot(xc, w['wx'][hp]) + w['bx'][:, sl])
    log_a = -LRU_C * _softplus(-w['lam'][:, sl]) * gr
    return jnp.exp(log_a), jnp.sqrt(1.0 - jnp.exp(2.0 * log_a)) * (gi * xc)


_RW_NAMES = ('mu', 'w0', 'w2', 'a0', 'a2', 'g2', 'k_k', 'k_a', 'r_k', 'ln_g', 'ln_b')
_LRU_NAMES = ('cw', 'cb', 'wa', 'ba', 'wx', 'bx', 'lam')


def _stack(x, m0):
    return jnp.concatenate([jnp.where(m0, x, 0.0), jnp.where(m0, 0.0, x)], axis=0)


def _mixer_prompt_kernel(*refs):
    n_w = len(_RW_NAMES) + len(_LRU_NAMES)
    prw_ref, puy_ref = refs[0], refs[1]
    w = dict(zip(_RW_NAMES + _LRU_NAMES, refs[2:2 + n_w]))
    yrw_ref, ylru_ref, sout_ref, hout_ref = refs[2 + n_w:6 + n_w]
    pbuf, ubuf, s_scr, h_scr = refs[6 + n_w:]
    C = CHUNK
    c = pl.program_id(1)

    @pl.when(c == 0)
    def _():
        pbuf[0:8, :] = jnp.zeros((8, N_SHIFT), F32)
        ubuf[0:8, :] = jnp.zeros((8, D_MODEL), F32)
        s_scr[...] = jnp.zeros_like(s_scr)
        h_scr[...] = jnp.zeros_like(h_scr)

    ones = _pair_ones()
    row = lax.broadcasted_iota(jnp.int32, (C, 1), 0)

    lru_groups = list(range(N_PAIR))

    def lru_next():
        if not lru_groups:
            return
        hp = lru_groups.pop(0)
        sl = slice(hp * LANES, (hp + 1) * LANES)
        u_in = puy_ref[:, sl]
        ubuf[8:8 + C, sl] = u_in
        cw = w['cw']
        xc = (w['cb'][:, sl] + ubuf[5:5 + C, sl] * cw[0:1, sl] + ubuf[6:6 + C, sl] * cw[1:2, sl]
              + ubuf[7:7 + C, sl] * cw[2:3, sl] + u_in * cw[3:4, sl])
        ubuf[0:8, sl] = u_in[C - 8:C, :]
        a_l, b_l = _lru_gates(xc, w, hp)
        d = 1
        while d < C:
            keep = row >= d
            a_sh = jnp.where(keep, pltpu.roll(a_l, d, 0), 1.0)
            b_sh = jnp.where(keep, pltpu.roll(b_l, d, 0), 0.0)
            b_l = a_l * b_sh + b_l
            a_l = a_l * a_sh
            d *= 2
        h = a_l * h_scr[0:1, sl] + b_l
        h_scr[0:1, sl] = h[C - 1:C, :]
        y_lru = h * _gelu_tanh(puy_ref[:, D_MODEL + hp * LANES:D_MODEL + (hp + 1) * LANES])
        ylru_ref[:, sl] = y_lru.astype(ylru_ref.dtype)

    p = prw_ref[...]
    pbuf[8:8 + C, :] = p
    prev = pbuf[7:7 + C, :]
    ps = p + w['mu'][...] * (prev - p)
    r, k_mod, v, ld, a_vec, b_vec, g = _rwkv_elem(ps, w, ones)
    pbuf[0:8, :] = p[C - 8:C, :]

    tri = jnp.where(lax.broadcasted_iota(jnp.int32, (C, C), 0) >= lax.broadcasted_iota(jnp.int32, (C, C), 1),
                    1.0, 0.0).astype(BF16)
    l_hi = ld.astype(BF16)
    l_r1 = ld - l_hi.astype(F32)
    cum = _dot(tri, l_hi) + _dot(tri, l_r1.astype(BF16))
    cum_last = cum[C - 1:C, :]
    e_neg = jnp.exp(-cum)
    at = a_vec * jnp.exp(cum - ld)
    rt = r * jnp.exp(cum)
    bt = b_vec * e_neg
    kt = k_mod * e_neg
    wc = jnp.exp(cum_last)
    bh = bt * wc
    kh = kt * wc
    bonus_dot = _segsum(r * k_mod * w['r_k'][...], ones)

    n = 2 * C
    ri = lax.broadcasted_iota(jnp.int32, (n, n), 0)
    ci = lax.broadcasted_iota(jnp.int32, (n, n), 1)
    low_strict = ri > ci
    low_incl = ri >= ci
    diag_blk = (ri // SUB) == (ci // SUB)
    eye = jnp.where(ri == ci, 1.0, 0.0)
    m0 = lax.broadcasted_iota(jnp.int32, (C, LANES), 1) < RW_HEAD

    pairs = range(N_PAIR)
    sls = [slice(hp * LANES, (hp + 1) * LANES) for hp in pairs]

    def wide(xs, ys_list):
        outs = [[None] * N_PAIR for _ in ys_list]
        for p in pairs:
            o = _dot(xs[p], jnp.concatenate([ys[p] for ys in ys_list], axis=1))
            for i in range(len(ys_list)):
                outs[i][p] = o[:, i * n:(i + 1) * n]
        return outs

    vs_ = [_stack(v[:, sl], m0) for sl in sls]
    s_bd = [s_scr[hp] for hp in pairs]
    gram = [_dot(jnp.concatenate([at[:, sl], rt[:, sl]], axis=0),
                 jnp.concatenate([_stack(bt[:, sl], m0), _stack(kt[:, sl], m0), s_bd[hp]], axis=0), _NT)
            for hp, sl in zip(pairs, sls)]
    lru_next()
    n_ab = [jnp.where(low_strict, _stack(gm[0:C, 0:n], m0), 0.0) for gm in gram]
    a_ak = [jnp.where(low_strict, _stack(gm[0:C, n:2 * n], m0), 0.0) for gm in gram]
    a_s = [_stack(gm[0:C, 2 * n:3 * n], m0) for gm in gram]
    r_bk = [jnp.concatenate([jnp.where(low_incl, _stack(gm[C:n, 0:n], m0), 0.0),
                             jnp.where(low_incl, _stack(gm[C:n, n:2 * n], m0), 0.0)], axis=1) for gm in gram]
    r_s = [_stack(gm[C:n, 2 * n:3 * n], m0) for gm in gram]
    rhs_u = [x + y for x, y in zip(a_s, wide(a_ak, [vs_])[0])]
    lru_next()

    n_d = [jnp.where(diag_blk, x, 0.0) for x in n_ab]
    n_o = [x - d for x, d in zip(n_ab, n_d)]
    d_inv = [eye + d for d in n_d]
    q = wide(n_d, [n_d])[0]
    lru_next()
    for _ in range(2):
        q, qd = wide(q, [q, d_inv])
        d_inv = [d + x for d, x in zip(d_inv, qd)]
        lru_next()
    d_inv = [d + x for d, x in zip(d_inv, wide(q, [d_inv])[0])]
    lru_next()
    m1, z = wide(d_inv, [n_o, rhs_u])
    lru_next()
    m2, mz = wide(m1, [m1, z])
    lru_next()
    z = [x + y for x, y in zip(z, mz)]
    u = [x + y for x, y in zip(z, wide(m2, [z])[0])]
    uv = [jnp.concatenate([x, vv], axis=0) for x, vv in zip(u, vs_)]
    y_s = [rs + _dot(rb, x) for rs, rb, x in zip(r_s, r_bk, uv)]
    lru_next()
    s_new = [s_bd[hp] * wc[:, sl]
             + _dot(uv[hp], jnp.concatenate([_stack(bh[:, sl], m0), _stack(kh[:, sl], m0)], axis=0), _TN)
             for hp, sl in zip(pairs, sls)]
    for hp in pairs:
        s_scr[hp] = s_new[hp]
    while lru_groups:
        lru_next()

    y = [x[0:C] + x[C:n] for x in y_s]
    mu = [_dot(x, ones) * (1.0 / RW_HEAD) for x in y]
    yc = [x - m for x, m in zip(y, mu)]
    var = [_dot(x * x, ones) * (1.0 / RW_HEAD) for x in yc]
    for hp, sl in zip(pairs, sls):
        yn = yc[hp] * lax.rsqrt(var[hp] + GN_EPS) * w['ln_g'][:, sl] + w['ln_b'][:, sl]
        yrw_ref[:, sl] = ((yn + bonus_dot[:, sl] * v[:, sl]) * g[:, sl]).astype(yrw_ref.dtype)

    @pl.when(c == pl.num_programs(1) - 1)
    def _():
        hout_ref[0] = h_scr[0:1, :]
        for hp in range(N_PAIR):
            s_bd = s_scr[hp]
            sout_ref[0, 2 * hp] = s_bd[0:RW_HEAD, 0:RW_HEAD]
            sout_ref[0, 2 * hp + 1] = s_bd[RW_HEAD:LANES, RW_HEAD:LANES]


def _mixer_prompt(p_rw, p_uy, weights, batch, seq):
    C = CHUNK
    nc = seq // C
    row_map = lambda b, c: (b * nc + c, 0)
    w_specs = [pl.BlockSpec(x.shape, (lambda b, c, nd=x.ndim: (0,) * nd)) for x in weights]
    return pl.pallas_call(
        _mixer_prompt_kernel,
        grid=(batch, nc),
        in_specs=[pl.BlockSpec((C, N_SHIFT), row_map), pl.BlockSpec((C, C_UY), row_map)] + w_specs,
        out_specs=[pl.BlockSpec((C, D_MODEL), row_map), pl.BlockSpec((C, D_MODEL), row_map),
                   pl.BlockSpec((1, RW_HEADS, RW_HEAD, RW_HEAD), lambda b, c: (b, 0, 0, 0)),
                   pl.BlockSpec((1, 1, D_MODEL), lambda b, c: (b, 0, 0))],
        out_shape=[jax.ShapeDtypeStruct((batch * seq, D_MODEL), BF16),
                   jax.ShapeDtypeStruct((batch * seq, D_MODEL), BF16),
                   jax.ShapeDtypeStruct((batch, RW_HEADS, RW_HEAD, RW_HEAD), F32),
                   jax.ShapeDtypeStruct((batch, 1, D_MODEL), F32)],
        scratch_shapes=[pltpu.VMEM((8 + C, N_SHIFT), F32), pltpu.VMEM((8 + C, D_MODEL), F32),
                        pltpu.VMEM((N_PAIR, LANES, LANES), F32), pltpu.VMEM((8, D_MODEL), F32)],
        compiler_params=_params(("parallel", "arbitrary")),
        name="mixer_prompt",
    )(p_rw, p_uy, *weights)


def _mixer_sample_prep_kernel(*refs):
    n_w = len(_RW_NAMES) + len(_LRU_NAMES)
    prw_ref, shift_ref, puy_ref, conv_ref, h0_ref = refs[0:5]
    w = dict(zip(_RW_NAMES + _LRU_NAMES, refs[5:5 + n_w]))
    rw_ref, ylru_ref, hnew_ref = refs[5 + n_w:]
    ones = _pair_ones()
    p = prw_ref[...]
    ps = p + w['mu'][...] * (shift_ref[...] - p)
    r, k_mod, v, ld, a_vec, b_vec, g = _rwkv_elem(ps, w, ones)
    for i, x in enumerate((r, k_mod, v, jnp.exp(ld), a_vec, b_vec, g)):
        rw_ref[i] = x
    cw = w['cw']
    for hp in range(N_PAIR):
        sl = slice(hp * LANES, (hp + 1) * LANES)
        prev = [conv_ref[:, j * D_MODEL + hp * LANES:j * D_MODEL + (hp + 1) * LANES] for j in range(CONV_W - 1)]
        xc = (w['cb'][:, sl] + prev[0] * cw[0:1, sl] + prev[1] * cw[1:2, sl] + prev[2] * cw[2:3, sl]
              + puy_ref[:, sl] * cw[3:4, sl])
        a_l, b_l = _lru_gates(xc, w, hp)
        h = a_l * h0_ref[:, sl] + b_l
        hnew_ref[:, sl] = h
        ylru_ref[:, sl] = h * _gelu_tanh(puy_ref[:, D_MODEL + hp * LANES:D_MODEL + (hp + 1) * LANES])


def _mixer_sample_prep(p_rw, shift, p_uy, conv, h0, weights):
    nb = p_rw.shape[0]
    ins = [p_rw, shift, p_uy, conv, h0] + list(weights)
    full = lambda x: pl.BlockSpec(x.shape, (lambda i, nd=x.ndim: (0,) * nd))
    return pl.pallas_call(
        _mixer_sample_prep_kernel,
        grid=(1,),
        in_specs=[full(x) for x in ins],
        out_specs=[pl.BlockSpec((7, nb, D_MODEL), lambda i: (0, 0, 0)),
                   pl.BlockSpec((nb, D_MODEL), lambda i: (0, 0)),
                   pl.BlockSpec((nb, D_MODEL), lambda i: (0, 0))],
        out_shape=[jax.ShapeDtypeStruct((7, nb, D_MODEL), F32),
                   jax.ShapeDtypeStruct((nb, D_MODEL), F32),
                   jax.ShapeDtypeStruct((nb, D_MODEL), F32)],
        compiler_params=_params(("arbitrary",)),
        name="mixer_sample_prep",
    )(*ins)


def _rwkv_step_kernel(*refs, n_carry):
    if n_carry:
        s_ref, x_ref, rk_ref, lng_ref, lnb_ref, carry_ref, sout_ref, y_ref = refs
        sout_ref[0:n_carry] = carry_ref[...]
    else:
        s_ref, x_ref, rk_ref, lng_ref, lnb_ref, sout_ref, y_ref = refs
    s = s_ref[0]
    r, k, v, wd, a, b, g = (x_ref[i, 0] for i in range(7))
    sa = jnp.sum(s * a[None, :, :], axis=1)
    s_new = s * wd[None, :, :] + sa[:, None, :] * b[None, :, :] + v[:, None, :] * k[None, :, :]
    sout_ref[n_carry, 0] = s_new
    y = jnp.sum(s_new * r[None, :, :], axis=1)
    mu = jnp.mean(y, axis=0, keepdims=True)
    yc = y - mu
    var = jnp.mean(yc * yc, axis=0, keepdims=True)
    yn = yc * lax.rsqrt(var + GN_EPS) * lng_ref[0] + lnb_ref[0]
    bonus = jnp.sum(r * k * rk_ref[0], axis=0, keepdims=True) * v
    y_ref[0] = (yn + bonus) * g


def _rwkv_step(state, l, rw7, r_k, ln_g, ln_b, carry):
    n = state.shape[-1]
    blk = (RW_HEAD, RW_HEAD, n)
    head = lambda *lead: pl.BlockSpec(lead + (RW_HEAD, 1), lambda i: (i, 0, 0))
    in_specs = [pl.BlockSpec((None, 1) + blk, lambda i: (l, i, 0, 0, 0)),
                pl.BlockSpec((7, 1, RW_HEAD, n), lambda i: (0, i, 0, 0)),
                head(1), head(1), head(1)]
    args = [state, rw7, r_k, ln_g, ln_b]
    if carry is not None:
        in_specs.append(pl.BlockSpec((l, 1) + blk, lambda i: (0, i, 0, 0, 0)))
        args.append(carry)
    return pl.pallas_call(
        functools.partial(_rwkv_step_kernel, n_carry=0 if carry is None else l),
        grid=(RW_HEADS,),
        in_specs=in_specs,
        out_specs=[pl.BlockSpec((l + 1, 1) + blk, lambda i: (0, i, 0, 0, 0)),
                   pl.BlockSpec((1, RW_HEAD, n), lambda i: (i, 0, 0))],
        out_shape=[jax.ShapeDtypeStruct((l + 1, RW_HEADS) + blk, F32),
                   jax.ShapeDtypeStruct((RW_HEADS, RW_HEAD, n), F32)],
        compiler_params=_params(("parallel",)),
        name="rwkv_step",
    )(*args)


def _mm_kernel(x_ref, w_ref, o_ref):
    o_ref[...] = _dot(x_ref[...], w_ref[...])


def _mm(x, w, l, tm, tn, name):
    m, kd = x.shape
    n = w.shape[2]
    return pl.pallas_call(
        _mm_kernel,
        grid=(m // tm, n // tn),
        in_specs=[pl.BlockSpec((tm, kd), lambda i, j: (i, 0)),
                  pl.BlockSpec((None, kd, tn), lambda i, j: (l, 0, j))],
        out_specs=pl.BlockSpec((tm, tn), lambda i, j: (i, j)),
        out_shape=jax.ShapeDtypeStruct((m, n), F32),
        compiler_params=_params(("parallel", "parallel")),
        name=name,
    )(x, w)


_PROJ_COLS = (0, N_SHIFT, N_SHIFT + C_UY, N_SHIFT + C_UY + 2 * D_MODEL)


def _proj3_kernel(x_ref, w_ref, prw_ref, puy_ref, pg_ref):
    x = x_ref[...].astype(BF16)
    for o_ref, c0, c1 in zip((prw_ref, puy_ref, pg_ref), _PROJ_COLS[:-1], _PROJ_COLS[1:]):
        o_ref[...] = _dot(x, w_ref[:, c0:c1])


def _proj3(x, l, w_in, tm):
    m = x.shape[0]
    rows = lambda width: pl.BlockSpec((tm, width), lambda i: (i, 0))
    widths = [c1 - c0 for c0, c1 in zip(_PROJ_COLS[:-1], _PROJ_COLS[1:])]
    return pl.pallas_call(
        _proj3_kernel,
        grid=(m // tm,),
        in_specs=[rows(D_MODEL), _layer_spec(w_in, l)],
        out_specs=[rows(n) for n in widths],
        out_shape=[jax.ShapeDtypeStruct((m, n), F32) for n in widths],
        compiler_params=_params(("parallel",)),
        name="proj_in",
    )(x, w_in)


def _post_kernel(x_ref, yrw_ref, ylru_ref, pg_ref, wrw_ref, wlru_ref, wmix_ref, gb_ref, g_ref, b_ref, wq_ref,
                 o_ref, q_ref):
    o_rw = _dot(yrw_ref[...], wrw_ref[...])
    o_lru = _dot(ylru_ref[...], wlru_ref[...])
    gates = _sigmoid(pg_ref[...] + gb_ref[...])
    mix = _dot(gates[:, 0:D_MODEL] * o_rw + gates[:, D_MODEL:2 * D_MODEL] * o_lru, wmix_ref[...])
    x1 = _layer_norm(ALPHA * x_ref[...] + mix, g_ref[...], b_ref[...])
    o_ref[...] = x1
    q_ref[...] = _dot(x1, wq_ref[...]).astype(q_ref.dtype)


def _layer_spec(a, l):
    return pl.BlockSpec((None,) + a.shape[1:], lambda *_: (l, 0, 0))


def _post(x, y_rw, y_lru, p_gate, l, w_rw, w_lru, w_mix, gate_b, g, b, wq, tm, q_dtype):
    m = x.shape[0]
    rows = lambda width: pl.BlockSpec((tm, width), lambda i: (i, 0))
    const = lambda a: _layer_spec(a, l)
    return pl.pallas_call(
        _post_kernel,
        grid=(m // tm,),
        in_specs=[rows(D_MODEL), rows(D_MODEL), rows(D_MODEL), rows(2 * D_MODEL),
                  const(w_rw), const(w_lru), const(w_mix), const(gate_b), const(g), const(b), const(wq)],
        out_specs=[rows(D_MODEL), rows(D_MODEL)],
        out_shape=[jax.ShapeDtypeStruct((m, D_MODEL), F32), jax.ShapeDtypeStruct((m, D_MODEL), q_dtype)],
        compiler_params=_params(("parallel",)),
        name="post",
    )(x, y_rw, y_lru, p_gate, w_rw, w_lru, w_mix, gate_b, g, b, wq)


def _attn_kernel(q_ref, k_ref, v_ref, o_ref):
    heads = [slice(h * XA_HEAD, (h + 1) * XA_HEAD) for h in range(XA_HEADS)]
    s = [_dot(q_ref[0, :, sl], k_ref[0, :, sl], _NT) * (XA_HEAD ** -0.5) for sl in heads]
    e = [jnp.exp(x - jnp.max(x, -1, keepdims=True)) for x in s]
    prob = [x / jnp.sum(x, -1, keepdims=True) for x in e]
    for sl, x in zip(heads, prob):
        o_ref[0, :, sl] = _dot(x, v_ref[0, :, sl]).astype(o_ref.dtype)


def _attn(q, k, v, tq):
    b, t, _ = q.shape
    return pl.pallas_call(
        _attn_kernel,
        grid=(b, t // tq),
        in_specs=[pl.BlockSpec((1, tq, D_MODEL), lambda i, j: (i, j, 0)),
                  pl.BlockSpec((1, N_MEM, D_MODEL), lambda i, j: (i, 0, 0)),
                  pl.BlockSpec((1, N_MEM, D_MODEL), lambda i, j: (i, 0, 0))],
        out_specs=pl.BlockSpec((1, tq, D_MODEL), lambda i, j: (i, j, 0)),
        out_shape=jax.ShapeDtypeStruct(q.shape, BF16),
        compiler_params=_params(("parallel", "parallel")),
        name="attn",
    )(q, k, v)


def _attn_decode_kernel(q_ref, k_ref, v_ref, o_ref):
    q = q_ref[...] * (XA_HEAD ** -0.5)
    part = jnp.sum(k_ref[...] * q[:, None, :, :], axis=-1, keepdims=True)
    s = part + pltpu.roll(part, XA_HEADS, 2)
    e = jnp.exp(s - jnp.max(s, axis=1, keepdims=True))
    o_ref[...] = jnp.sum(e * v_ref[...], axis=1) / jnp.sum(e, axis=1)


def _attn_decode(q, k, v, l, nb):
    n = q.shape[0]
    halves = XA_HEAD // LANES
    rows = halves * XA_HEADS

    def view(x):
        lead = x.shape[:-2]
        x = x.reshape(lead + (XA_HEADS, halves, LANES))
        return jnp.swapaxes(x, -3, -2).reshape(lead + (rows, LANES))

    kv = pl.BlockSpec((None, nb, N_MEM, rows, LANES), lambda i: (l, i, 0, 0, 0))
    qo = pl.BlockSpec((nb, rows, LANES), lambda i: (i, 0, 0))
    o = pl.pallas_call(
        _attn_decode_kernel,
        grid=(n // nb,),
        in_specs=[qo, kv, kv],
        out_specs=qo,
        out_shape=jax.ShapeDtypeStruct((n, rows, LANES), F32),
        compiler_params=_params(("parallel",)),
        name="attn_decode",
    )(view(q), view(k), view(v))
    return jnp.swapaxes(o.reshape(n, halves, XA_HEADS, LANES), 1, 2).reshape(n, XA_HEADS, XA_HEAD)


def _tail_kernel(a_ref, x_ref, wo_ref, g2_ref, b2_ref, up_ref, down_ref, g3_ref, b3_ref, o_ref):
    x2 = _layer_norm(ALPHA * x_ref[...] + _dot(a_ref[...], wo_ref[...]), g2_ref[...], b2_ref[...])
    hdn = jnp.square(jnp.maximum(_dot(x2, up_ref[...]), 0.0))
    o_ref[...] = _layer_norm(ALPHA * x2 + _dot(hdn, down_ref[...]), g3_ref[...], b3_ref[...])


def _tail(a, x, l, wo, g2, b2, up, down, g3, b3, tm):
    m = a.shape[0]
    rows = pl.BlockSpec((tm, D_MODEL), lambda i: (i, 0))
    const = lambda z: _layer_spec(z, l)
    return pl.pallas_call(
        _tail_kernel,
        grid=(m // tm,),
        in_specs=[rows, rows, const(wo), const(g2), const(b2), const(up), const(down), const(g3), const(b3)],
        out_specs=rows,
        out_shape=jax.ShapeDtypeStruct((m, D_MODEL), F32),
        compiler_params=_params(("parallel",)),
        name="tail",
    )(a, x, wo, g2, b2, up, down, g3, b3)


def _row(x):
    return x.reshape(1, -1)


def _pair_blocks(wb):
    z = jnp.zeros((N_PAIR, RW_HEAD, RW_HEAD), wb.dtype)
    even, odd = wb[0::2], wb[1::2]
    top = jnp.concatenate([even, z], axis=2)
    bot = jnp.concatenate([z, odd], axis=2)
    return jnp.concatenate([top, bot], axis=1).astype(BF16)


def _layer_weights(l, W):
    z64 = jnp.zeros((64, D_MODEL), F32)
    rw = [_row(W['mu_shift'][l]), _row(W['rw_w0'][l]),
          jnp.concatenate([W['rw_w2'][l], z64], 0).astype(BF16), _row(W['rw_a0'][l]),
          jnp.concatenate([z64, W['rw_a2'][l]], 0).astype(BF16), W['rw_g2'][l].astype(BF16),
          _row(W['rw_k_k'][l]), _row(W['rw_k_a'][l]), _row(W['rw_r_k'][l]),
          _row(W['rw_lnx_g'][l]), _row(W['rw_lnx_b'][l])]
    lru = [W['lru_conv_w'][l], _row(W['lru_conv_b'][l]), _pair_blocks(W['lru_wa'][l]), _row(W['lru_ba'][l]),
           _pair_blocks(W['lru_wx'][l]), _row(W['lru_bx'][l]), _row(W['lru_lambda'][l])]
    return rw + lru


def _dense_weights(W):
    vec = lambda a: a.reshape(DEPTH, 1, -1)
    w_in = W['w_in']
    return dict(
        w_in=w_in.astype(BF16),
        rw_proj=W['rw_proj'].astype(BF16), lru_proj=W['lru_proj'].astype(BF16), w_mix=W['w_out_mix'].astype(BF16),
        gate_b=vec(W['mix_gate_b']), wq=W['xa_wq'].astype(BF16), wo=W['xa_wo'].astype(BF16),
        up=W['mlp_up'].astype(BF16), down=W['mlp_down'].astype(BF16),
        ln1_g=vec(W['ln1_g']), ln1_b=vec(W['ln1_b']), ln2_g=vec(W['ln2_g']), ln2_b=vec(W['ln2_b']),
        ln3_g=vec(W['ln3_g']), ln3_b=vec(W['ln3_b']))


def _trunk(x, mem_k, mem_v, states, W, P, tm, tq):
    batch, seq, _ = x.shape
    m = batch * seq
    x2 = x.reshape(m, D_MODEL)
    o_rw_st, o_shift_st, o_h_st, o_conv_st = [], [], [], []
    s_new_prev = None
    if states is not None:
        states = (jnp.transpose(states[0], (0, 2, 3, 4, 1)),) + tuple(states[1:])
    for l in range(DEPTH):
        mixw = _layer_weights(l, W)
        p_rw, p_uy, p_gate = _proj3(x2, l, P['w_in'], tm)
        if states is None:
            y_rw, y_lru, s_new, h_new = _mixer_prompt(p_rw, p_uy, mixw, batch, seq)
            h_new = h_new.reshape(batch, D_MODEL)
            conv_new = p_uy.reshape(batch, seq, C_UY)[:, seq - (CONV_W - 1):, 0:D_MODEL]
        else:
            s_rw, s_shift, s_h, s_conv = states
            rw7, y_lru, h_new = _mixer_sample_prep(
                p_rw, s_shift[l], p_uy, s_conv[l].reshape(batch, (CONV_W - 1) * D_MODEL), s_h[l], mixw)
            hd = (RW_HEADS, RW_HEAD, 1)
            rw7 = jnp.transpose(rw7, (0, 2, 1)).reshape(7, RW_HEADS, RW_HEAD, batch)
            s_new, y_rw = _rwkv_step(s_rw, l, rw7, W['rw_r_k'][l].reshape(hd), W['rw_lnx_g'][l].reshape(hd),
                                     W['rw_lnx_b'][l].reshape(hd), s_new_prev)
            s_new_prev = s_new
            y_rw = y_rw.reshape(D_MODEL, batch).T
            conv_new = jnp.concatenate([s_conv[l][:, 1:], p_uy[:, None, 0:D_MODEL]], axis=1)
        x2, q = _post(x2, y_rw, y_lru, p_gate, l, P['rw_proj'], P['lru_proj'], P['w_mix'], P['gate_b'],
                      P['ln1_g'], P['ln1_b'], P['wq'], min(tm, 512), BF16 if states is None else F32)
        if states is None:
            o = _attn(q.reshape(batch, seq, D_MODEL), mem_k[l], mem_v[l], tq)
        else:
            o = _attn_decode(q.reshape(batch, XA_HEADS, XA_HEAD), mem_k, mem_v, l, 4)
        x2 = _tail(o.reshape(m, D_MODEL), x2, l, P['wo'], P['ln2_g'], P['ln2_b'], P['up'], P['down'],
                   P['ln3_g'], P['ln3_b'], tm)
        o_rw_st.append(s_new)
        o_shift_st.append(p_rw.reshape(batch, seq, N_SHIFT)[:, seq - 1])
        o_h_st.append(h_new)
        o_conv_st.append(conv_new)
    rw_state = jnp.stack(o_rw_st, 0) if states is None else jnp.transpose(s_new_prev, (0, 4, 1, 2, 3))
    return x2.reshape(batch, seq, D_MODEL), (rw_state, jnp.stack(o_shift_st, 0),
                                             jnp.stack(o_h_st, 0), jnp.stack(o_conv_st, 0))


def kernel(x_prompt, x_sample, mem_prompt, cache_mem_k, cache_mem_v, state_rwkv, state_rwkv_shift, state_lru_h, state_lru_conv, w_in, mu_shift, rw_w0, rw_w2, rw_a0, rw_a2, rw_g2, rw_k_k, rw_k_a, rw_r_k, rw_lnx_g, rw_lnx_b, rw_proj, lru_conv_w, lru_conv_b, lru_wa, lru_ba, lru_wx, lru_bx, lru_lambda, lru_proj, mix_gate_b, w_out_mix, ln1_g, ln1_b, xa_wq, xa_wk, xa_wv, xa_wo, ln2_g, ln2_b, mlp_up, mlp_down, ln3_g, ln3_b):
    W = dict(w_in=w_in, mu_shift=mu_shift, rw_w0=rw_w0, rw_w2=rw_w2, rw_a0=rw_a0, rw_a2=rw_a2,
             rw_g2=rw_g2, rw_k_k=rw_k_k, rw_k_a=rw_k_a, rw_r_k=rw_r_k, rw_lnx_g=rw_lnx_g,
             rw_lnx_b=rw_lnx_b, rw_proj=rw_proj, lru_conv_w=lru_conv_w, lru_conv_b=lru_conv_b,
             lru_wa=lru_wa, lru_ba=lru_ba, lru_wx=lru_wx, lru_bx=lru_bx, lru_lambda=lru_lambda,
             lru_proj=lru_proj, mix_gate_b=mix_gate_b, w_out_mix=w_out_mix, ln1_g=ln1_g, ln1_b=ln1_b,
             xa_wq=xa_wq, xa_wo=xa_wo, ln2_g=ln2_g, ln2_b=ln2_b, mlp_up=mlp_up, mlp_down=mlp_down,
             ln3_g=ln3_g, ln3_b=ln3_b)
    bp, seq, _ = x_prompt.shape
    bs = x_sample.shape[0]
    mem2 = mem_prompt.reshape(bp * N_MEM, D_MODEL)
    tmem = min(512, bp * N_MEM)
    wk, wv = xa_wk.astype(BF16), xa_wv.astype(BF16)
    mem_k_p = jnp.stack([_mm(mem2, wk, l, tmem, 1024, "mem_k") for l in range(DEPTH)], 0)
    mem_v_p = jnp.stack([_mm(mem2, wv, l, tmem, 1024, "mem_v") for l in range(DEPTH)], 0)
    mem_k_p = mem_k_p.reshape(DEPTH, bp, N_MEM, D_MODEL)
    mem_v_p = mem_v_p.reshape(DEPTH, bp, N_MEM, D_MODEL)
    P = _dense_weights(W)
    tm_p = min(512, bp * seq)
    y_prompt, (p_rw, p_shift, p_h, p_conv) = _trunk(x_prompt, mem_k_p, mem_v_p, None, W, P, tm_p, min(512, seq))
    y_sample, (s_rw, s_shift, s_h, s_conv) = _trunk(
        x_sample, cache_mem_k, cache_mem_v, (state_rwkv, state_rwkv_shift, state_lru_h, state_lru_conv),
        W, P, bs, 1)
    kv_shape = (DEPTH, bp, N_MEM, XA_HEADS, XA_HEAD)
    return (y_prompt, y_sample, p_rw, p_shift, p_h, p_conv, mem_k_p.reshape(kv_shape), mem_v_p.reshape(kv_shape),
            s_rw, s_shift, s_h, s_conv)
```
